```python
import math
import numpy as np
import jax
import jax.numpy as jnp
from jax import lax

D_MODEL = 1024
BATCH = 4
SEQ = 8192
DEPTH = 2

GRID_W = 64
CTX_LEN = 256
HEAD_DIM = 64
BLOCK_Q = 128
ROPE_THETA = 10000.0
EPS = 1e-6
NEG_INF = -1e30
A_HEADS = 8
A_KV_HEADS = 2
B_HEADS = 8
NA_ROWS = 8
NA_COLS = 16
C_HEADS = 4
D_HEADS = 8
D_KV_HEADS = 2
D_WINDOW = 128
MIX_EVEN = (A_HEADS + B_HEADS) * HEAD_DIM
MIX_ODD = C_HEADS * 2 * HEAD_DIM + D_HEADS * HEAD_DIM
EVEN_SPLITS = (A_HEADS * HEAD_DIM, A_KV_HEADS * HEAD_DIM, A_KV_HEADS * HEAD_DIM,
               B_HEADS * HEAD_DIM, B_HEADS * HEAD_DIM, B_HEADS * HEAD_DIM, MIX_EVEN)
ODD_SPLITS = (C_HEADS * 2 * HEAD_DIM, C_HEADS * 2 * HEAD_DIM, C_HEADS * 2 * HEAD_DIM,
              D_HEADS * HEAD_DIM, D_KV_HEADS * HEAD_DIM, D_KV_HEADS * HEAD_DIM, MIX_ODD)
IN_EVEN = sum(EVEN_SPLITS)
IN_ODD = sum(ODD_SPLITS)
N_EVEN = (DEPTH + 1) // 2
N_ODD = DEPTH // 2
SCALE = HEAD_DIM ** -0.5

kernel_name = 'hybrid_prefix_dit_block'


def rms_norm(x, gain=None):
    xf = x.astype(jnp.float32)
    y = xf * lax.rsqrt(jnp.mean(xf * xf, axis=-1, keepdims=True) + EPS)
    if gain is not None:
        y = y * gain.astype(jnp.float32)
    return y.astype(x.dtype)


def _split(x, sizes):
    idx = [int(i) for i in np.cumsum(sizes)[:-1]]
    return jnp.split(x, idx, axis=-1)


def _modulation(cvec, w_mod, b_mod):
    return jnp.split(jax.nn.silu(cvec) @ w_mod + b_mod, 3, axis=-1)


def lambda_init(layer):
    return 0.8 - 0.6 * math.exp(-0.3 * layer)


def axial_rope_tables(n_tok):
    t = jnp.arange(n_tok, dtype=jnp.int32)
    row = (t // GRID_W).astype(jnp.float32)
    col = (t % GRID_W).astype(jnp.float32)
    quarter = HEAD_DIM // 4
    inv_freq = ROPE_THETA ** (-jnp.arange(quarter, dtype=jnp.float32) / quarter)
    ang_r = row[:, None] * inv_freq
    ang_c = col[:, None] * inv_freq
    return (jnp.cos(ang_r), jnp.sin(ang_r), jnp.cos(ang_c), jnp.sin(ang_c))


def _rot(x, cos, sin):
    x1, x2 = jnp.split(x, 2, axis=-1)
    cos = cos[:, None, :]
    sin = sin[:, None, :]
    return jnp.concatenate([x1 * cos - x2 * sin, x1 * sin + x2 * cos], axis=-1)


def apply_axial_rope(x, tabs):
    cr, sr, cc, sc = tabs
    xr, xcol = jnp.split(x, 2, axis=-1)
    return jnp.concatenate([_rot(xr, cr, sr), _rot(xcol, cc, sc)], axis=-1).astype(x.dtype)


def neighbourhood_tables(n_tok):
    rows = n_tok // GRID_W
    win_r = min(NA_ROWS, rows)
    t = jnp.arange(n_tok, dtype=jnp.int32)
    r = t // GRID_W
    col = t % GRID_W
    rs = jnp.clip(r - win_r // 2, 0, rows - win_r)
    cs = jnp.clip(col - NA_COLS // 2, 0, GRID_W - NA_COLS)
    kr = rs[:, None, None] + jnp.arange(win_r, dtype=jnp.int32)[None, :, None]
    kc = cs[:, None, None] + jnp.arange(NA_COLS, dtype=jnp.int32)[None, None, :]
    key_idx = (kr * GRID_W + kc).reshape(n_tok, win_r * NA_COLS)
    bias_idx = ((kr - r[:, None, None] + NA_ROWS - 1) * (2 * NA_COLS - 1)
                + (kc - col[:, None, None] + NA_COLS - 1)).reshape(n_tok, win_r * NA_COLS)
    return key_idx, bias_idx


def _to_blocks(x):
    b, s = x.shape[:2]
    return jnp.moveaxis(x.reshape((b, s // BLOCK_Q, BLOCK_Q) + x.shape[2:]), 1, 0)


def _from_blocks(y):
    nb, b = y.shape[:2]
    return jnp.moveaxis(y, 0, 1).reshape((b, nb * BLOCK_Q) + y.shape[3:])


def gqa_attend(q5, k, v):
    sc = jnp.einsum('bqkgd,bskd->bkgqs', q5, k).astype(jnp.float32) * SCALE
    p = jax.nn.softmax(sc, axis=-1).astype(v.dtype)
    return jnp.einsum('bkgqs,bskd->bqkgd', p, v)


def diff_attend(q, k, v, lam):
    sc = jnp.einsum('bqhmd,bkhmd->bhmqk', q, k).astype(jnp.float32) * SCALE
    p = jax.nn.softmax(sc, axis=-1)
    a = (p[:, :, 0] - lam * p[:, :, 1]).astype(v.dtype)
    return jnp.einsum('bhqk,bkhe->bqhe', a, v)


def sink_attend(q5, k, v, sink):
    sc = jnp.einsum('bqkgd,bckd->bkgqc', q5, k).astype(jnp.float32) * SCALE
    s_sink = jnp.broadcast_to(sink[None, :, :, None, None], sc.shape[:-1] + (1,))
    p = jax.nn.softmax(jnp.concatenate([s_sink, sc], axis=-1), axis=-1)[..., 1:]
    return jnp.einsum('bkgqc,bckd->bqkgd', p.astype(v.dtype), v)


def neighbourhood_attend(q, k, v, k_ctx, v_ctx, rpb, key_idx, bias_idx):
    b, n_tok, h, d = q.shape
    n_nb = key_idx.shape[-1]
    rpb_flat = rpb.reshape(h, -1).astype(jnp.float32)

    def blk(args):
        qi, ki, bi = args
        kg = k[:, ki]
        vg = v[:, ki]
        s_n = jnp.einsum('bqhd,bqnhd->bhqn', qi, kg).astype(jnp.float32) * SCALE + rpb_flat[:, bi][None]
        s_c = jnp.einsum('bqhd,bchd->bhqc', qi, k_ctx).astype(jnp.float32) * SCALE
        p = jax.nn.softmax(jnp.concatenate([s_n, s_c], axis=-1), axis=-1).astype(v.dtype)
        return (jnp.einsum('bhqn,bqnhd->bqhd', p[..., :n_nb], vg)
                + jnp.einsum('bhqc,bchd->bqhd', p[..., n_nb:], v_ctx))

    nb = n_tok // BLOCK_Q
    out = lax.map(blk, (_to_blocks(q), key_idx.reshape(nb, BLOCK_Q, n_nb), bias_idx.reshape(nb, BLOCK_Q, n_nb)))
    return _from_blocks(out).reshape(b, n_tok, h * d)


def window_attend(q5, k, v, k_ctx, v_ctx, sink):
    b, n_tok, hkv, g, d = q5.shape
    pad = D_WINDOW
    span = BLOCK_Q + 2 * pad
    kp = jnp.pad(k, ((0, 0), (pad, pad), (0, 0), (0, 0)))
    vp = jnp.pad(v, ((0, 0), (pad, pad), (0, 0), (0, 0)))
    starts = jnp.arange(n_tok // BLOCK_Q, dtype=jnp.int32) * BLOCK_Q

    def blk(args):
        qi, st = args
        kw = lax.dynamic_slice_in_dim(kp, st, span, axis=1)
        vw = lax.dynamic_slice_in_dim(vp, st, span, axis=1)
        qpos = st + jnp.arange(BLOCK_Q, dtype=jnp.int32)
        kpos = st - pad + jnp.arange(span, dtype=jnp.int32)
        valid = ((kpos[None, :] >= 0) & (kpos[None, :] < n_tok)
                 & (jnp.abs(qpos[:, None] - kpos[None, :]) <= D_WINDOW))
        s_w = jnp.einsum('bqkgd,bskd->bkgqs', qi, kw).astype(jnp.float32) * SCALE
        s_w = jnp.where(valid, s_w, NEG_INF)
        s_c = jnp.einsum('bqkgd,bckd->bkgqc', qi, k_ctx).astype(jnp.float32) * SCALE
        s_sink = jnp.broadcast_to(sink[None, :, :, None, None], s_w.shape[:-1] + (1,))
        p = jax.nn.softmax(jnp.concatenate([s_sink, s_w, s_c], axis=-1), axis=-1).astype(v.dtype)
        return (jnp.einsum('bkgqs,bskd->bqkgd', p[..., 1:1 + span], vw)
                + jnp.einsum('bkgqc,bckd->bqkgd', p[..., 1 + span:], v_ctx))

    out = lax.map(blk, (_to_blocks(q5), starts))
    return _from_blocks(out).reshape(b, n_tok, hkv * g * d)


def even_mixer(h, hc, w_in, q_gain, k_gain, rpb, rope_tabs, na_tabs, ctx_out):
    b, n_tok, _ = h.shape
    L = hc.shape[1]
    hd = HEAD_DIM
    ga = A_HEADS // A_KV_HEADS
    qa, ka, va, qb, kb, vb, z = _split(h @ w_in, EVEN_SPLITS)
    qa_c, ka_c, va_c, qb_c, kb_c, vb_c, z_c = _split(hc @ w_in, EVEN_SPLITS)
    qa = apply_axial_rope(rms_norm(qa.reshape(b, n_tok, A_HEADS, hd), q_gain), rope_tabs)
    ka = apply_axial_rope(rms_norm(ka.reshape(b, n_tok, A_KV_HEADS, hd), k_gain), rope_tabs)
    ka_c = rms_norm(ka_c.reshape(b, L, A_KV_HEADS, hd), k_gain)
    va_c = va_c.reshape(b, L, A_KV_HEADS, hd)
    k_all = jnp.concatenate([ka, ka_c], axis=1)
    v_all = jnp.concatenate([va.reshape(b, n_tok, A_KV_HEADS, hd), va_c], axis=1)
    qa5 = qa.reshape(b, n_tok, A_KV_HEADS, ga, hd)
    ya = _from_blocks(lax.map(lambda qi: gqa_attend(qi, k_all, v_all), _to_blocks(qa5))).reshape(b, n_tok, -1)
    kb_c = kb_c.reshape(b, L, B_HEADS, hd)
    vb_c = vb_c.reshape(b, L, B_HEADS, hd)
    yb = neighbourhood_attend(qb.reshape(b, n_tok, B_HEADS, hd), kb.reshape(b, n_tok, B_HEADS, hd),
                              vb.reshape(b, n_tok, B_HEADS, hd), kb_c, vb_c, rpb, na_tabs[0], na_tabs[1])
    y = jnp.concatenate([ya, yb], axis=-1) * jax.nn.silu(z)
    if not ctx_out:
        return y, None
    qa_c5 = rms_norm(qa_c.reshape(b, L, A_KV_HEADS, ga, hd), q_gain)
    ya_c = gqa_attend(qa_c5, ka_c, va_c).reshape(b, L, -1)
    yb_c = gqa_attend(qb_c.reshape(b, L, B_HEADS, 1, hd), kb_c, vb_c).reshape(b, L, -1)
    y_ctx = jnp.concatenate([ya_c, yb_c], axis=-1) * jax.nn.silu(z_c)
    return y, y_ctx


def odd_mixer(h, hc, w_in, lam_vecs, subln, sinks, lam0, rope_tabs, ctx_out):
    b, n_tok, _ = h.shape
    L = hc.shape[1]
    hd = HEAD_DIM
    gd = D_HEADS // D_KV_HEADS
    qc, kc, vc, qd, kd, vd, z = _split(h @ w_in, ODD_SPLITS)
    qc_c, kc_c, vc_c, qd_c, kd_c, vd_c, z_c = _split(hc @ w_in, ODD_SPLITS)
    lv = lam_vecs.astype(jnp.float32)
    lam = jnp.exp(jnp.sum(lv[0] * lv[1])) - jnp.exp(jnp.sum(lv[2] * lv[3])) + lam0
    qc = apply_axial_rope(qc.reshape(b, n_tok, 2 * C_HEADS, hd), rope_tabs).reshape(b, n_tok, C_HEADS, 2, hd)
    kc = apply_axial_rope(kc.reshape(b, n_tok, 2 * C_HEADS, hd), rope_tabs).reshape(b, n_tok, C_HEADS, 2, hd)
    kc_c = kc_c.reshape(b, L, C_HEADS, 2, hd)
    vc_c = vc_c.reshape(b, L, C_HEADS, 2 * hd)
    kc_all = jnp.concatenate([kc, kc_c], axis=1)
    vc_all = jnp.concatenate([vc.reshape(b, n_tok, C_HEADS, 2 * hd), vc_c], axis=1)
    yc = _from_blocks(lax.map(lambda qi: diff_attend(qi, kc_all, vc_all, lam), _to_blocks(qc)))
    yc = (rms_norm(yc, subln) * (1.0 - lam0)).reshape(b, n_tok, -1)
    qd5 = apply_axial_rope(qd.reshape(b, n_tok, D_HEADS, hd), rope_tabs).reshape(b, n_tok, D_KV_HEADS, gd, hd)
    kd = apply_axial_rope(kd.reshape(b, n_tok, D_KV_HEADS, hd), rope_tabs)
    kd_c = kd_c.reshape(b, L, D_KV_HEADS, hd)
    vd_c = vd_c.reshape(b, L, D_KV_HEADS, hd)
    sink = sinks.reshape(D_KV_HEADS, gd).astype(jnp.float32)
    yd = window_attend(qd5, kd, vd.reshape(b, n_tok, D_KV_HEADS, hd), kd_c, vd_c, sink)
    y = jnp.concatenate([yc, yd], axis=-1) * jax.nn.silu(z)
    if not ctx_out:
        return y, None
    yc_c = (rms_norm(diff_attend(qc_c.reshape(b, L, C_HEADS, 2, hd), kc_c, vc_c, lam), subln)
            * (1.0 - lam0)).reshape(b, L, -1)
    yd_c = sink_attend(qd_c.reshape(b, L, D_KV_HEADS, gd, hd), kd_c, vd_c, sink).reshape(b, L, -1)
    y_ctx = jnp.concatenate([yc_c, yd_c], axis=-1) * jax.nn.silu(z_c)
    return y, y_ctx


def setup_inputs(seed: int = 0) -> dict:
    key = jax.random.key(seed)
    ks = jax.random.split(key, 17)
    f32 = jnp.float32

    def nrm(k, shape, s):
        return jax.random.normal(k, shape, f32) * s

    return {
        'x': nrm(ks[0], (BATCH, SEQ, D_MODEL), 1.0),
        'c': nrm(ks[1], (BATCH, D_MODEL), 1.0),
        'ctx': nrm(ks[2], (BATCH, CTX_LEN, D_MODEL), 1.0),
        'c_ctx': nrm(ks[3], (D_MODEL,), 1.0),
        'w_mod': nrm(ks[4], (DEPTH, D_MODEL, 3 * D_MODEL), 0.5 * D_MODEL ** -0.5),
        'b_mod': nrm(ks[5], (DEPTH, 3 * D_MODEL), 0.02),
        'w_in_even': nrm(ks[6], (N_EVEN, D_MODEL, IN_EVEN), D_MODEL ** -0.5),
        'w_out_even': nrm(ks[7], (N_EVEN, MIX_EVEN, D_MODEL), MIX_EVEN ** -0.5),
        'a_q_norm': 1.0 + nrm(ks[8], (N_EVEN, HEAD_DIM), 0.02),
        'a_k_norm': 1.0 + nrm(ks[9], (N_EVEN, HEAD_DIM), 0.02),
        'b_rpb': nrm(ks[10], (N_EVEN, B_HEADS, 2 * NA_ROWS - 1, 2 * NA_COLS - 1), 0.1),
        'w_in_odd': nrm(ks[11], (N_ODD, D_MODEL, IN_ODD), D_MODEL ** -0.5),
        'w_out_odd': nrm(ks[12], (N_ODD, MIX_ODD, D_MODEL), MIX_ODD ** -0.5),
        'c_lambda': nrm(ks[13], (N_ODD, 4, HEAD_DIM), 0.1),
        'c_subln': 1.0 + nrm(ks[14], (N_ODD, 2 * HEAD_DIM), 0.02),
        'd_sinks': nrm(ks[15], (N_ODD, D_HEADS), 0.5),
        'final_norm': 1.0 + nrm(ks[16], (D_MODEL,), 0.02),
    }


def reference(x, c, ctx, c_ctx, w_mod, b_mod, w_in_even, w_out_even, a_q_norm, a_k_norm, b_rpb,
              w_in_odd, w_out_odd, c_lambda, c_subln, d_sinks, final_norm):
    n_tok = x.shape[1]
    rope_tabs = axial_rope_tables(n_tok)
    na_tabs = neighbourhood_tables(n_tok)
    xc = ctx
    for l in range(DEPTH):
        last = l == DEPTH - 1
        shift, scale, gate = _modulation(c, w_mod[l], b_mod[l])
        shift_c, scale_c, gate_c = _modulation(c_ctx, w_mod[l], b_mod[l])
        h = rms_norm(x) * (1.0 + scale[:, None]) + shift[:, None]
        hc = rms_norm(xc) * (1.0 + scale_c) + shift_c
        i = l // 2
        if l % 2 == 0:
            y, y_ctx = even_mixer(h, hc, w_in_even[i], a_q_norm[i], a_k_norm[i], b_rpb[i],
                                  rope_tabs, na_tabs, not last)
            w_out = w_out_even[i]
        else:
            y, y_ctx = odd_mixer(h, hc, w_in_odd[i], c_lambda[i], c_subln[i], d_sinks[i],
                                 lambda_init(l), rope_tabs, not last)
            w_out = w_out_odd[i]
        x = x + gate[:, None] * (y @ w_out)
        if not last:
            xc = xc + gate_c * (y_ctx @ w_out)
    return rms_norm(x, final_norm)
```

```python
import functools
import math

import numpy as np
import jax
import jax.numpy as jnp
from jax import lax
from jax.experimental import pallas as pl
from jax.experimental.pallas import tpu as pltpu

D_MODEL = 1024
BATCH = 4
SEQ = 8192
DEPTH = 2
GRID_W = 64
CTX_LEN = 256
HEAD_DIM = 64
ROPE_THETA = 10000.0
EPS = 1e-6
NEG_INF = -1e30
A_HEADS = 8
B_HEADS = 8
NA_ROWS = 8
NA_COLS = 16
C_HEADS = 4
D_HEADS = 8
D_WINDOW = 128
SCALE = HEAD_DIM ** -0.5
LOG2E = math.log2(math.e)
QSCALE = SCALE * LOG2E

LANES = 128
HALF = HEAD_DIM
TOK = SEQ + CTX_LEN
TM = 256
N_LAT_TM = SEQ // TM
TQ_BLK = 128
N_LAT_TQ = SEQ // TQ_BLK
N_CTX_TQ = CTX_LEN // TQ_BLK
GRID_ROWS = SEQ // GRID_W
NB_ROWS = 10
NB_KEYS = NB_ROWS * GRID_W
NB_BLKS = NB_KEYS // TQ_BLK
VMEM_LIMIT = 56 * 1024 * 1024

HEAD_PERM = (0, 4, 1, 5, 2, 6, 3, 7)
F32 = jnp.float32
BF16 = jnp.bfloat16


def _lambda_init(layer):
    return 0.8 - 0.6 * math.exp(-0.3 * layer)


def _lane_iota(rows):
    return lax.broadcasted_iota(jnp.int32, (rows, LANES), 1)


def _silu(z):
    return z / (1.0 + jnp.exp(-z))


def _mod_kernel(c_ref, w_ref, b_ref, o_ref):
    o_ref[0] = jnp.dot(_silu(c_ref[...]), w_ref[0], preferred_element_type=F32) + b_ref[0]


def _modulation(cs, w_mod, b_mod):
    tn = D_MODEL
    return pl.pallas_call(
        _mod_kernel,
        out_shape=jax.ShapeDtypeStruct((DEPTH, 8, 3 * D_MODEL), F32),
        grid=(DEPTH, 3 * D_MODEL // tn),
        in_specs=[
            pl.BlockSpec((8, D_MODEL), lambda l, j: (0, 0)),
            pl.BlockSpec((1, D_MODEL, tn), lambda l, j: (l, 0, j)),
            pl.BlockSpec((1, 1, tn), lambda l, j: (l, 0, j)),
        ],
        out_specs=pl.BlockSpec((1, 8, tn), lambda l, j: (l, 0, j)),
        compiler_params=pltpu.CompilerParams(
            dimension_semantics=("arbitrary", "arbitrary"), vmem_limit_bytes=VMEM_LIMIT),
        name="modulation",
    )(cs, w_mod, b_mod.reshape(DEPTH, 1, 3 * D_MODEL))


def _swap16(x):
    lane = _lane_iota(x.shape[0])
    fwd = pltpu.roll(x, LANES - 16, 1)
    bwd = pltpu.roll(x, 16, 1)
    return jnp.where((lane & 16) == 0, fwd, bwd)


def _proj_kernel(*refs, plan, dual):
    if dual:
        x_ref, xc_ref = refs[:2]
        refs = refs[2:]
    else:
        x_ref = refs[0]
        refs = refs[1:]
    mod_ref, w_ref, qg_ref, kg_ref, cos_ref, sin_ref, seg_ref = refs[:7]
    out_refs = refs[7:]

    xt = x_ref[0]
    if dual:
        xt = jnp.where(pl.program_id(1) < N_LAT_TM, xt, xc_ref[0])
    ms = jnp.mean(xt * xt, axis=-1, keepdims=True)
    xn = xt * lax.rsqrt(ms + EPS)
    shift = mod_ref[0, :, 0:D_MODEL]
    scale = mod_ref[0, :, D_MODEL:2 * D_MODEL]
    h = (xn * (1.0 + scale) + shift).astype(BF16)

    cos = cos_ref[...]
    sin = sin_ref[...]
    seg = seg_ref[...]
    col = 0
    for width, outs in plan:
        acc = jnp.dot(h, w_ref[:, col:col + width], preferred_element_type=F32)
        col += width
        for out_idx, lo, hi, norm, rope, scale_q in outs:
            pieces = []
            for c0 in range(lo, hi, LANES):
                blk = acc[:, c0:c0 + LANES]
                if norm:
                    sq = blk * blk
                    sq_hi = sq.astype(BF16)
                    sq_lo = (sq - sq_hi.astype(F32)).astype(BF16)
                    ssum = (jnp.dot(sq_hi, seg, preferred_element_type=F32)
                            + jnp.dot(sq_lo, seg, preferred_element_type=F32))
                    gain = qg_ref[...] if norm == "q" else kg_ref[...]
                    blk = blk * lax.rsqrt(ssum * (1.0 / HEAD_DIM) + EPS) * gain
                if rope:
                    blk = blk * cos + _swap16(blk) * sin
                if scale_q:
                    blk = blk * QSCALE
                pieces.append(blk.astype(BF16))
            out_refs[out_idx][0] = pieces[0] if len(pieces) == 1 else jnp.concatenate(pieces, axis=-1)


def _in_proj(x, xc, mod, w, qg, kg, cos, sin, seg, plan, out_widths):
    dual = xc is not None
    n_tiles = TOK // TM
    if dual:
        x_specs = [
            pl.BlockSpec((1, TM, D_MODEL), lambda b, i: (b, jnp.minimum(i, N_LAT_TM - 1), 0)),
            pl.BlockSpec((1, CTX_LEN, D_MODEL), lambda b, i: (b, 0, 0)),
        ]
        xs = (x, xc)
    else:
        x_specs = [pl.BlockSpec((1, TM, D_MODEL), lambda b, i: (b, i, 0))]
        xs = (x,)
    n_in = w.shape[1]
    in_specs = x_specs + [
        pl.BlockSpec((1, 1, 3 * D_MODEL), lambda b, i: (jnp.where(i < N_LAT_TM, b, BATCH), 0, 0)),
        pl.BlockSpec((D_MODEL, n_in), lambda b, i: (0, 0)),
        pl.BlockSpec((1, LANES), lambda b, i: (0, 0)),
        pl.BlockSpec((1, LANES), lambda b, i: (0, 0)),
        pl.BlockSpec((TM, LANES), lambda b, i: (i, 0)),
        pl.BlockSpec((TM, LANES), lambda b, i: (i, 0)),
        pl.BlockSpec((LANES, LANES), lambda b, i: (0, 0)),
    ]
    out_shape = tuple(jax.ShapeDtypeStruct((BATCH, TOK, wd), BF16) for wd in out_widths)
    out_specs = tuple(pl.BlockSpec((1, TM, wd), lambda b, i: (b, i, 0)) for wd in out_widths)
    return pl.pallas_call(
        functools.partial(_proj_kernel, plan=plan, dual=dual),
        out_shape=out_shape,
        grid=(BATCH, n_tiles),
        in_specs=in_specs,
        out_specs=out_specs,
        compiler_params=pltpu.CompilerParams(
            dimension_semantics=("arbitrary", "arbitrary"), vmem_limit_bytes=VMEM_LIMIT),
        name="in_proj",
    )(*xs, mod, w, qg, kg, cos, sin, seg)


def _out_kernel(*refs, dual, final):
    if dual:
        x_ref, xc_ref = refs[:2]
        refs = refs[2:]
    else:
        x_ref = refs[0]
        refs = refs[1:]
    y1_ref, y2_ref, z_ref, w_ref, mod_ref = refs[:5]
    refs = refs[5:]
    if final:
        fn_ref, o_ref = refs
    else:
        (o_ref,) = refs

    xt = x_ref[0]
    if dual:
        xt = jnp.where(pl.program_id(1) < N_LAT_TM, xt, xc_ref[0])
    y = jnp.concatenate([y1_ref[0], y2_ref[0]], axis=-1).astype(F32)
    g = (y * _silu(z_ref[0].astype(F32))).astype(BF16)
    d = jnp.dot(g, w_ref[...], preferred_element_type=F32)
    gate = mod_ref[0, :, 2 * D_MODEL:3 * D_MODEL]
    xo = xt + gate * d
    if final:
        ms = jnp.mean(xo * xo, axis=-1, keepdims=True)
        xo = xo * lax.rsqrt(ms + EPS) * fn_ref[...]
    o_ref[0] = xo


def _out_proj(x, xc, y1, y2, z, w, mod, final_gain):
    dual = xc is not None
    final = final_gain is not None
    n_tok = SEQ if final else TOK
    half = y1.shape[-1]
    if dual:
        x_specs = [
            pl.BlockSpec((1, TM, D_MODEL), lambda b, i: (b, jnp.minimum(i, N_LAT_TM - 1), 0)),
            pl.BlockSpec((1, CTX_LEN, D_MODEL), lambda b, i: (b, 0, 0)),
        ]
        xs = (x, xc)
    else:
        x_specs = [pl.BlockSpec((1, TM, D_MODEL), lambda b, i: (b, i, 0))]
        xs = (x,)
    in_specs = x_specs + [
        pl.BlockSpec((1, TM, half), lambda b, i: (b, i, 0)),
        pl.BlockSpec((1, TM, half), lambda b, i: (b, i, 0)),
        pl.BlockSpec((1, TM, 2 * half), lambda b, i: (b, i, 0)),
        pl.BlockSpec((2 * half, D_MODEL), lambda b, i: (0, 0)),
        pl.BlockSpec((1, 1, 3 * D_MODEL), lambda b, i: (jnp.where(i < N_LAT_TM, b, BATCH), 0, 0)),
    ]
    args = xs + (y1, y2, z, w, mod)
    if final:
        in_specs.append(pl.BlockSpec((1, D_MODEL), lambda b, i: (0, 0)))
        args = args + (final_gain,)
    return pl.pallas_call(
        functools.partial(_out_kernel, dual=dual, final=final),
        out_shape=jax.ShapeDtypeStruct((BATCH, n_tok, D_MODEL), F32),
        grid=(BATCH, n_tok // TM),
        in_specs=in_specs,
        out_specs=pl.BlockSpec((1, TM, D_MODEL), lambda b, i: (b, i, 0)),
        compiler_params=pltpu.CompilerParams(
            dimension_semantics=("arbitrary", "arbitrary"), vmem_limit_bytes=VMEM_LIMIT),
        name="out_proj",
    )(*args)


def _expand_heads(q_ref, lhs_scr, nq, tq):
    lane = _lane_iota(tq)
    for j in range(nq):
        qj = q_ref[0, :, j * LANES:(j + 1) * LANES].astype(F32)
        lhs_scr[(2 * j) * tq:(2 * j + 1) * tq, :] = jnp.where(lane < HALF, qj, 0.0).astype(BF16)
        lhs_scr[(2 * j + 1) * tq:(2 * j + 2) * tq, :] = jnp.where(lane >= HALF, qj, 0.0).astype(BF16)


def _scores(lhs, k):
    return lax.dot_general(lhs, k, (((1,), (1,)), ((), ())), preferred_element_type=F32)


def _merge_heads(o, tq, j):
    lane = _lane_iota(tq)
    return jnp.where(lane < HALF, o[(2 * j) * tq:(2 * j + 1) * tq], o[(2 * j + 1) * tq:(2 * j + 2) * tq])


def _dense_kernel(*refs, nq, tq, tk, n_lat_tiles, diff, lam0):
    if diff:
        q_ref, k_ref, v_ref, lamv_ref, subln_ref, o_ref, lhs_scr, m_scr, l_scr, acc_scr = refs
    else:
        q_ref, k_ref, v_ref, o_ref, lhs_scr, m_scr, l_scr, acc_scr = refs
    m_rows = 2 * nq * tq
    _expand_heads(q_ref, lhs_scr, nq, tq)
    m_scr[...] = jnp.full((m_rows, LANES), -jnp.inf, F32)
    l_scr[...] = jnp.zeros((m_rows, LANES), F32)
    acc_scr[...] = jnp.zeros((m_rows, LANES), F32)

    def step(start, size):
        kc = k_ref[0, pl.ds(start, size), :]
        vc = v_ref[0, pl.ds(start, size), :]
        s = _scores(lhs_scr[...], kc)
        m_prev = m_scr[...]
        m_new = jnp.maximum(m_prev, jnp.max(s, axis=-1, keepdims=True))
        alpha = jnp.exp2(m_prev - m_new)
        p = jnp.exp2(s - pltpu.repeat(m_new, size // LANES, axis=1))
        l_scr[...] = alpha * l_scr[...] + jnp.sum(p, axis=-1, keepdims=True)
        acc_scr[...] = alpha * acc_scr[...] + jnp.dot(p.astype(BF16), vc, preferred_element_type=F32)
        m_scr[...] = m_new

    def latent_keys():
        def body(c, carry):
            step(pl.multiple_of(c * tk, tk), tk)
            return carry
        lax.fori_loop(0, SEQ // tk, body, 0)

    if n_lat_tiles is None:
        latent_keys()
    else:
        pl.when(pl.program_id(2) < n_lat_tiles)(latent_keys)
    step(SEQ, CTX_LEN)

    o = acc_scr[...] / l_scr[...]
    if diff:
        lv = lamv_ref[...]
        lam = (jnp.exp(jnp.sum(lv[0:1] * lv[1:2], axis=-1, keepdims=True))
               - jnp.exp(jnp.sum(lv[2:3] * lv[3:4], axis=-1, keepdims=True)) + lam0)
        dlt = o[0:tq] - lam * o[tq:2 * tq]
        ms = jnp.mean(dlt * dlt, axis=-1, keepdims=True)
        y = dlt * lax.rsqrt(ms + EPS) * subln_ref[...] * (1.0 - lam0)
        o_ref[0] = y.astype(BF16)
    else:
        pieces = [_merge_heads(o, tq, j).astype(BF16) for j in range(nq)]
        o_ref[0] = pieces[0] if nq == 1 else jnp.concatenate(pieces, axis=-1)


def _dense_attn(q, k, v, *, nq, tq, tk, n_q_tok, diff=False, lamv=None, subln=None, lam0=0.0):
    groups = q.shape[-1] // (nq * LANES)
    n_tiles = n_q_tok // tq
    n_lat_tiles = None if n_q_tok == SEQ else SEQ // tq
    m_rows = 2 * nq * tq
    in_specs = [
        pl.BlockSpec((1, tq, nq * LANES), lambda b, g, i: (b, i, g)),
        pl.BlockSpec((1, TOK, LANES), lambda b, g, i: (b, 0, g)),
        pl.BlockSpec((1, TOK, LANES), lambda b, g, i: (b, 0, g)),
    ]
    args = (q, k, v)
    if diff:
        in_specs += [pl.BlockSpec((4, HEAD_DIM), lambda b, g, i: (0, 0)),
                     pl.BlockSpec((1, LANES), lambda b, g, i: (0, 0))]
        args = args + (lamv, subln)
    return pl.pallas_call(
        functools.partial(_dense_kernel, nq=nq, tq=tq, tk=tk, n_lat_tiles=n_lat_tiles,
                          diff=diff, lam0=lam0),
        out_shape=jax.ShapeDtypeStruct((BATCH, n_q_tok, groups * nq * LANES), BF16),
        grid=(BATCH, groups, n_tiles),
        in_specs=in_specs,
        out_specs=pl.BlockSpec((1, tq, nq * LANES), lambda b, g, i: (b, i, g)),
        scratch_shapes=[pltpu.VMEM((m_rows, LANES), BF16),
                        pltpu.VMEM((m_rows, LANES), F32),
                        pltpu.VMEM((m_rows, LANES), F32),
                        pltpu.VMEM((m_rows, LANES), F32)],
        compiler_params=pltpu.CompilerParams(
            dimension_semantics=("arbitrary", "arbitrary", "arbitrary"),
            vmem_limit_bytes=VMEM_LIMIT),
        name="dense_diff_attn" if diff else "dense_gqa_attn",
    )(*args)


def _nb_kernel(*refs):
    q_ref = refs[0]
    k_refs = refs[1:1 + NB_BLKS]
    v_refs = refs[1 + NB_BLKS:1 + 2 * NB_BLKS]
    kc_ref, vc_ref, bias_ref, o_ref, lhs_scr = refs[1 + 2 * NB_BLKS:]
    tq = TQ_BLK
    nq = B_HEADS // 2
    _expand_heads(q_ref, lhs_scr, nq, tq)
    pieces = []
    for j in range(nq):
        cs = slice(j * LANES, (j + 1) * LANES)
        lhs = lhs_scr[(2 * j) * tq:(2 * j + 2) * tq, :]
        kcat = jnp.concatenate([r[0, :, cs] for r in k_refs], axis=0)
        vcat = jnp.concatenate([r[0, :, cs] for r in v_refs], axis=0)
        s_n = _scores(lhs, kcat) + bias_ref[0, j]
        s_c = _scores(lhs, kc_ref[0, :, cs])
        m = jnp.maximum(jnp.max(s_n, axis=-1, keepdims=True), jnp.max(s_c, axis=-1, keepdims=True))
        p_n = jnp.exp2(s_n - m)
        p_c = jnp.exp2(s_c - m)
        l = jnp.sum(p_n, axis=-1, keepdims=True) + jnp.sum(p_c, axis=-1, keepdims=True)
        o = (jnp.dot(p_n.astype(BF16), vcat, preferred_element_type=F32)
             + jnp.dot(p_c.astype(BF16), vc_ref[0, :, cs], preferred_element_type=F32)) / l
        pieces.append(_merge_heads(o, tq, 0).astype(BF16))
    o_ref[0] = jnp.concatenate(pieces, axis=-1)


def _nb_case(i):
    last = N_LAT_TQ - 1
    return jnp.where(i == 0, 0, jnp.where(i == 1, 1, jnp.where(
        i == last - 1, 3, jnp.where(i == last, 4, jnp.where(i > last, 5, 2)))))


def _nb_attn(q, k, v, bias):
    width = B_HEADS * HEAD_DIM
    n_tiles = N_LAT_TQ + N_CTX_TQ

    def kv_spec(t):
        return pl.BlockSpec((1, TQ_BLK, width),
                            lambda b, i: (b, jnp.clip(i - 2, 0, N_LAT_TQ - NB_BLKS) + t, 0))

    in_specs = ([pl.BlockSpec((1, TQ_BLK, width), lambda b, i: (b, i, 0))]
                + [kv_spec(t) for t in range(NB_BLKS)]
                + [kv_spec(t) for t in range(NB_BLKS)]
                + [pl.BlockSpec((1, CTX_LEN, width), lambda b, i: (b, SEQ // CTX_LEN, 0)),
                   pl.BlockSpec((1, CTX_LEN, width), lambda b, i: (b, SEQ // CTX_LEN, 0)),
                   pl.BlockSpec((1, B_HEADS // 2, 2 * TQ_BLK, NB_KEYS),
                                lambda b, i: (_nb_case(i), 0, 0, 0))])
    return pl.pallas_call(
        _nb_kernel,
        out_shape=jax.ShapeDtypeStruct((BATCH, TOK, width), BF16),
        grid=(BATCH, n_tiles),
        in_specs=in_specs,
        out_specs=pl.BlockSpec((1, TQ_BLK, width), lambda b, i: (b, i, 0)),
        scratch_shapes=[pltpu.VMEM((B_HEADS * TQ_BLK, LANES), BF16)],
        compiler_params=pltpu.CompilerParams(
            dimension_semantics=("arbitrary", "arbitrary"), vmem_limit_bytes=VMEM_LIMIT),
        name="neighbourhood_attn",
    )(q, *([k] * NB_BLKS), *([v] * NB_BLKS), k, v, bias)


def _nb_bias_table(rpb):
    qr = np.arange(TQ_BLK) // GRID_W
    qc = np.arange(TQ_BLK) % GRID_W
    krow = np.arange(NB_KEYS) // GRID_W
    kc = np.arange(NB_KEYS) % GRID_W
    idx_cases, valid_cases = [], []
    for i in (0, 1, 2, N_LAT_TQ - 2, N_LAT_TQ - 1):
        base = int(np.clip(2 * i - 4, 0, GRID_ROWS - NB_ROWS))
        r = 2 * i + qr
        kr = base + krow
        rs = np.clip(r - NA_ROWS // 2, 0, GRID_ROWS - NA_ROWS)
        cs = np.clip(qc - NA_COLS // 2, 0, GRID_W - NA_COLS)
        dr = kr[None, :] - r[:, None]
        dc = kc[None, :] - qc[:, None]
        valid = ((kr[None, :] >= rs[:, None]) & (kr[None, :] < rs[:, None] + NA_ROWS)
                 & (kc[None, :] >= cs[:, None]) & (kc[None, :] < cs[:, None] + NA_COLS))
        idx = (dr + NA_ROWS - 1) * (2 * NA_COLS - 1) + (dc + NA_COLS - 1)
        idx_cases.append(np.where(valid, idx, 0))
        valid_cases.append(valid)
    idx_all = jnp.asarray(np.stack(idx_cases), jnp.int32)
    valid_all = jnp.asarray(np.stack(valid_cases))
    rpb_flat = rpb.reshape(B_HEADS, -1).astype(F32)
    tab = jnp.where(valid_all[None], rpb_flat[:, idx_all] * LOG2E, NEG_INF)
    tab = jnp.concatenate([tab, jnp.full((B_HEADS, 1, TQ_BLK, NB_KEYS), NEG_INF, F32)], axis=1)
    tab = tab.reshape(B_HEADS // 2, 2, 6, TQ_BLK, NB_KEYS)
    return jnp.transpose(tab, (2, 0, 1, 3, 4)).reshape(6, B_HEADS // 2, 2 * TQ_BLK, NB_KEYS)


def _win_kernel(sink_ref, q_ref, kp_ref, kq_ref, kn_ref, vp_ref, vq_ref, vn_ref, kc_ref, vc_ref,
                o_ref, lhs_scr):
    tq = TQ_BLK
    nq = D_HEADS // 2
    i = pl.program_id(1)
    _expand_heads(q_ref, lhs_scr, nq, tq)
    kcat = jnp.concatenate([kp_ref[0], kq_ref[0], kn_ref[0], kc_ref[0]], axis=0)
    vcat = jnp.concatenate([vp_ref[0], vq_ref[0], vn_ref[0], vc_ref[0]], axis=0)
    span = 3 * tq
    n_keys = span + CTX_LEN
    rr = lax.broadcasted_iota(jnp.int32, (tq, n_keys), 0)
    cc = lax.broadcasted_iota(jnp.int32, (tq, n_keys), 1)
    rel = cc - tq - rr
    kpos = i * tq - tq + cc
    valid = (cc >= span) | ((jnp.abs(rel) <= D_WINDOW) & (kpos >= 0) & (kpos < SEQ))
    s = _scores(lhs_scr[...], kcat)
    outs = []
    for g in range(2 * nq):
        head = HEAD_PERM[g]
        sink = sink_ref[head]
        sg = jnp.where(valid, s[g * tq:(g + 1) * tq], NEG_INF)
        m = jnp.maximum(jnp.max(sg, axis=-1, keepdims=True), sink)
        p = jnp.exp2(sg - m)
        l = jnp.sum(p, axis=-1, keepdims=True) + jnp.exp2(sink - m)
        outs.append(jnp.dot(p.astype(BF16), vcat, preferred_element_type=F32) / l)
    o = jnp.concatenate(outs, axis=0)
    o_ref[0] = jnp.concatenate([_merge_heads(o, tq, j).astype(BF16) for j in range(nq)], axis=-1)


def _win_attn(q, k, v, sinks):
    width = D_HEADS * HEAD_DIM
    last = N_LAT_TQ - 1

    def kv_spec(off):
        return pl.BlockSpec((1, TQ_BLK, LANES), lambda b, i, s: (b, jnp.clip(i + off, 0, last), 0))

    ctx_spec = pl.BlockSpec((1, CTX_LEN, LANES), lambda b, i, s: (b, SEQ // CTX_LEN, 0))
    grid_spec = pltpu.PrefetchScalarGridSpec(
        num_scalar_prefetch=1,
        grid=(BATCH, N_LAT_TQ),
        in_specs=[pl.BlockSpec((1, TQ_BLK, width), lambda b, i, s: (b, i, 0)),
                  kv_spec(-1), kv_spec(0), kv_spec(1),
                  kv_spec(-1), kv_spec(0), kv_spec(1),
                  ctx_spec, ctx_spec],
        out_specs=pl.BlockSpec((1, TQ_BLK, width), lambda b, i, s: (b, i, 0)),
        scratch_shapes=[pltpu.VMEM((D_HEADS * TQ_BLK, LANES), BF16)],
    )
    return pl.pallas_call(
        _win_kernel,
        out_shape=jax.ShapeDtypeStruct((BATCH, SEQ, width), BF16),
        grid_spec=grid_spec,
        compiler_params=pltpu.CompilerParams(
            dimension_semantics=("arbitrary", "arbitrary"), vmem_limit_bytes=VMEM_LIMIT),
        name="window_attn",
    )(sinks, q, k, k, k, v, v, v, k, v)


def _rope_tables():
    t = jnp.arange(SEQ, dtype=jnp.int32)
    row = (t // GRID_W).astype(F32)
    col = (t % GRID_W).astype(F32)
    quarter = HEAD_DIM // 4
    inv_freq = ROPE_THETA ** (-jnp.arange(quarter, dtype=F32) / quarter)
    ang_r = row[:, None] * inv_freq
    ang_c = col[:, None] * inv_freq
    cr, sr, cc, sc = jnp.cos(ang_r), jnp.sin(ang_r), jnp.cos(ang_c), jnp.sin(ang_c)
    cos = jnp.tile(jnp.concatenate([cr, cr, cc, cc], axis=-1), (1, LANES // HEAD_DIM))
    sin = jnp.tile(jnp.concatenate([-sr, sr, -sc, sc], axis=-1), (1, LANES // HEAD_DIM))
    cos = jnp.concatenate([cos, jnp.ones((CTX_LEN, LANES), F32)], axis=0)
    sin = jnp.concatenate([sin, jnp.zeros((CTX_LEN, LANES), F32)], axis=0)
    return cos, sin


def _head_cols(perm):
    return np.concatenate([np.arange(HEAD_DIM) + HEAD_DIM * h for h in perm])


def kernel(x, c, ctx, c_ctx, w_mod, b_mod, w_in_even, w_out_even, a_q_norm, a_k_norm, b_rpb,
           w_in_odd, w_out_odd, c_lambda, c_subln, d_sinks, final_norm):
    assert DEPTH == 2 and x.shape == (BATCH, SEQ, D_MODEL) and ctx.shape == (BATCH, CTX_LEN, D_MODEL)
    perm = _head_cols(HEAD_PERM)
    hw = A_HEADS * HEAD_DIM

    cs = jnp.concatenate([c, c_ctx[None], jnp.zeros((8 - BATCH - 1, D_MODEL), F32)], axis=0)
    mod = _modulation(cs, w_mod, b_mod)
    mod0 = mod[0].reshape(8, 1, 3 * D_MODEL)
    mod1 = mod[1].reshape(8, 1, 3 * D_MODEL)

    cos, sin = _rope_tables()
    seg = jnp.asarray(np.kron(np.eye(LANES // HEAD_DIM), np.ones((HEAD_DIM, HEAD_DIM))), BF16)
    tile2 = LANES // HEAD_DIM
    qg = jnp.tile(a_q_norm[0].astype(F32), tile2).reshape(1, LANES)
    kg = jnp.tile(a_k_norm[0].astype(F32), tile2).reshape(1, LANES)

    z0 = 2304
    cols0 = np.concatenate([perm, np.arange(hw, z0), z0 + perm, np.arange(z0 + hw, z0 + 2 * hw)])
    w_in0 = w_in_even[0][:, cols0].astype(BF16)
    w_out0 = w_out_even[0][np.concatenate([perm, np.arange(hw, 2 * hw)])].astype(BF16)
    plan0 = (
        (hw, ((0, 0, hw, "q", True, True),)),
        (2 * LANES, ((1, 0, LANES, "k", True, False), (2, LANES, 2 * LANES, None, False, False))),
        (hw, ((3, 0, hw, None, False, True),)),
        (hw, ((4, 0, hw, None, False, False),)),
        (hw, ((5, 0, hw, None, False, False),)),
        (2 * hw, ((6, 0, 2 * hw, None, False, False),)),
    )
    qa, ka, va, qb, kb, vb, zz = _in_proj(x, ctx, mod0, w_in0, qg, kg, cos, sin, seg, plan0,
                                          (hw, LANES, LANES, hw, hw, hw, 2 * hw))
    ya = _dense_attn(qa, ka, va, nq=4, tq=128, tk=512, n_q_tok=TOK)
    yb = _nb_attn(qb, kb, vb, _nb_bias_table(b_rpb[0]))
    x1 = _out_proj(x, ctx, ya, yb, zz, w_out0, mod0, None)

    cols1 = np.concatenate([np.arange(0, 3 * hw), 3 * hw + perm, np.arange(4 * hw, z0 + hw), z0 + hw + perm])
    w_in1 = w_in_odd[0][:, cols1].astype(BF16)
    w_out1 = w_out_odd[0][np.concatenate([np.arange(hw), hw + perm])].astype(BF16)
    plan1 = (
        (hw, ((0, 0, hw, None, True, True),)),
        (hw, ((1, 0, hw, None, True, False),)),
        (hw, ((2, 0, hw, None, False, False),)),
        (hw, ((3, 0, hw, None, True, True),)),
        (2 * LANES, ((4, 0, LANES, None, True, False), (5, LANES, 2 * LANES, None, False, False))),
        (2 * hw, ((6, 0, 2 * hw, None, False, False),)),
    )
    qc, kc, vc, qd, kd, vd, zz1 = _in_proj(x1, None, mod1, w_in1, qg, kg, cos, sin, seg, plan1,
                                           (hw, hw, hw, hw, LANES, LANES, 2 * hw))
    yc = _dense_attn(qc, kc, vc, nq=1, tq=512, tk=512, n_q_tok=SEQ, diff=True,
                     lamv=c_lambda[0].astype(F32), subln=c_subln[0].astype(F32).reshape(1, LANES),
                     lam0=_lambda_init(1))
    yd = _win_attn(qd, kd, vd, d_sinks[0].astype(F32) * LOG2E)
    return _out_proj(x1, None, yc, yd, zz1, w_out1, mod1, final_norm.astype(F32).reshape(1, D_MODEL))
```

```python
import functools
import math

import numpy as np
import jax
import jax.numpy as jnp
from jax import lax
from jax.experimental import pallas as pl
from jax.experimental.pallas import tpu as pltpu

D_MODEL = 1024
BATCH = 4
SEQ = 8192
DEPTH = 2
GRID_W = 64
CTX_LEN = 256
HEAD_DIM = 64
ROPE_THETA = 10000.0
EPS = 1e-6
NEG_INF = -1e30
A_HEADS = 8
B_HEADS = 8
NA_ROWS = 8
NA_COLS = 16
C_HEADS = 4
D_HEADS = 8
D_WINDOW = 128
SCALE = HEAD_DIM ** -0.5
LOG2E = math.log2(math.e)
QSCALE = SCALE * LOG2E

LANES = 128
HALF = HEAD_DIM
TOK = SEQ + CTX_LEN
TM = 256
N_LAT_TM = SEQ // TM
QB = 256
TQ_BLK = 128
N_LAT_TQ = SEQ // TQ_BLK
N_CTX_TQ = CTX_LEN // TQ_BLK
GRID_ROWS = SEQ // GRID_W
NB_ROWS = 10
NB_KEYS = NB_ROWS * GRID_W
NB_BLKS = NB_KEYS // TQ_BLK
VMEM_LIMIT = 56 * 1024 * 1024

HEAD_PERM = (0, 4, 1, 5, 2, 6, 3, 7)
F32 = jnp.float32
BF16 = jnp.bfloat16


def _lambda_init(layer):
    return 0.8 - 0.6 * math.exp(-0.3 * layer)


def _lane_iota(rows):
    return lax.broadcasted_iota(jnp.int32, (rows, LANES), 1)


def _silu(z):
    return z / (1.0 + jnp.exp(-z))


def _mod_kernel(c_ref, w_ref, b_ref, o_ref):
    o_ref[0] = jnp.dot(_silu(c_ref[...]), w_ref[0], preferred_element_type=F32) + b_ref[0]


def _modulation(cs, w_mod, b_mod):
    tn = D_MODEL
    return pl.pallas_call(
        _mod_kernel,
        out_shape=jax.ShapeDtypeStruct((DEPTH, 8, 3 * D_MODEL), F32),
        grid=(DEPTH, 3 * D_MODEL // tn),
        in_specs=[
            pl.BlockSpec((8, D_MODEL), lambda l, j: (0, 0)),
            pl.BlockSpec((1, D_MODEL, tn), lambda l, j: (l, 0, j)),
            pl.BlockSpec((1, 1, tn), lambda l, j: (l, 0, j)),
        ],
        out_specs=pl.BlockSpec((1, 8, tn), lambda l, j: (l, 0, j)),
        compiler_params=pltpu.CompilerParams(
            dimension_semantics=("arbitrary", "arbitrary"), vmem_limit_bytes=VMEM_LIMIT),
        name="modulation",
    )(cs, w_mod, b_mod.reshape(DEPTH, 1, 3 * D_MODEL))


def _swap16(x):
    lane = _lane_iota(x.shape[0])
    fwd = pltpu.roll(x, LANES - 16, 1)
    bwd = pltpu.roll(x, 16, 1)
    return jnp.where((lane & 16) == 0, fwd, bwd)


def _proj_kernel(*refs, plan, dual):
    if dual:
        x_ref, xc_ref = refs[:2]
        refs = refs[2:]
    else:
        x_ref = refs[0]
        refs = refs[1:]
    mod_ref, w_ref, qg_ref, kg_ref, cos_ref, sin_ref, seg_ref = refs[:7]
    out_refs = refs[7:]

    xt = x_ref[0]
    if dual:
        xt = jnp.where(pl.program_id(1) < N_LAT_TM, xt, xc_ref[0])
    ms = jnp.mean(xt * xt, axis=-1, keepdims=True)
    xn = xt * lax.rsqrt(ms + EPS)
    shift = mod_ref[0, :, 0:D_MODEL]
    scale = mod_ref[0, :, D_MODEL:2 * D_MODEL]
    h = (xn * (1.0 + scale) + shift).astype(BF16)

    cos = cos_ref[...]
    sin = sin_ref[...]
    seg = seg_ref[...]
    col = 0
    for width, outs in plan:
        acc = jnp.dot(h, w_ref[:, col:col + width], preferred_element_type=F32)
        col += width
        for out_idx, lo, hi, norm, rope, scale_q, transpose in outs:
            pieces = []
            for c0 in range(lo, hi, LANES):
                blk = acc[:, c0:c0 + LANES]
                if transpose:
                    out_refs[out_idx][0, (c0 - lo) // LANES, 0] = blk.T.astype(BF16)
                    continue
                if norm:
                    sq = blk * blk
                    sq_hi = sq.astype(BF16)
                    sq_lo = (sq - sq_hi.astype(F32)).astype(BF16)
                    ssum = (jnp.dot(sq_hi, seg, preferred_element_type=F32)
                            + jnp.dot(sq_lo, seg, preferred_element_type=F32))
                    gain = qg_ref[...] if norm == "q" else kg_ref[...]
                    blk = blk * lax.rsqrt(ssum * (1.0 / HEAD_DIM) + EPS) * gain
                if rope:
                    blk = blk * cos + _swap16(blk) * sin
                if scale_q:
                    blk = blk * QSCALE
                pieces.append(blk.astype(BF16))
            if pieces:
                out_refs[out_idx][0] = pieces[0] if len(pieces) == 1 else jnp.concatenate(pieces, axis=-1)


def _in_proj(x, xc, mod, w, qg, kg, cos, sin, seg, plan, out_widths):
    dual = xc is not None
    n_tiles = TOK // TM
    if dual:
        x_specs = [
            pl.BlockSpec((1, TM, D_MODEL), lambda b, i: (b, jnp.minimum(i, N_LAT_TM - 1), 0)),
            pl.BlockSpec((1, CTX_LEN, D_MODEL), lambda b, i: (b, 0, 0)),
        ]
        xs = (x, xc)
    else:
        x_specs = [pl.BlockSpec((1, TM, D_MODEL), lambda b, i: (b, i, 0))]
        xs = (x,)
    n_in = w.shape[1]
    in_specs = x_specs + [
        pl.BlockSpec((1, 1, 3 * D_MODEL), lambda b, i: (jnp.where(i < N_LAT_TM, b, BATCH), 0, 0)),
        pl.BlockSpec((D_MODEL, n_in), lambda b, i: (0, 0)),
        pl.BlockSpec((1, LANES), lambda b, i: (0, 0)),
        pl.BlockSpec((1, LANES), lambda b, i: (0, 0)),
        pl.BlockSpec((TM, LANES), lambda b, i: (i, 0)),
        pl.BlockSpec((TM, LANES), lambda b, i: (i, 0)),
        pl.BlockSpec((LANES, LANES), lambda b, i: (0, 0)),
    ]
    transposed = {o[0] for _, outs in plan for o in outs if o[6]}
    out_shape, out_specs = [], []
    for idx, wd in enumerate(out_widths):
        if idx in transposed:
            out_shape.append(jax.ShapeDtypeStruct((BATCH, wd // LANES, n_tiles, LANES, TM), BF16))
            out_specs.append(pl.BlockSpec((1, wd // LANES, 1, LANES, TM), lambda b, i: (b, 0, i, 0, 0)))
        else:
            out_shape.append(jax.ShapeDtypeStruct((BATCH, TOK, wd), BF16))
            out_specs.append(pl.BlockSpec((1, TM, wd), lambda b, i: (b, i, 0)))
    return pl.pallas_call(
        functools.partial(_proj_kernel, plan=plan, dual=dual),
        out_shape=tuple(out_shape),
        grid=(BATCH, n_tiles),
        in_specs=in_specs,
        out_specs=tuple(out_specs),
        compiler_params=pltpu.CompilerParams(
            dimension_semantics=("arbitrary", "arbitrary"), vmem_limit_bytes=VMEM_LIMIT),
        name="in_proj",
    )(*xs, mod, w, qg, kg, cos, sin, seg)


def _out_kernel(*refs, dual, final):
    if dual:
        x_ref, xc_ref = refs[:2]
        refs = refs[2:]
    else:
        x_ref = refs[0]
        refs = refs[1:]
    y1_ref, y2_ref, z_ref, w_ref, mod_ref = refs[:5]
    refs = refs[5:]
    if final:
        fn_ref, o_ref = refs
    else:
        (o_ref,) = refs

    xt = x_ref[0]
    if dual:
        xt = jnp.where(pl.program_id(1) < N_LAT_TM, xt, xc_ref[0])
    y = jnp.concatenate([y1_ref[0], y2_ref[0]], axis=-1).astype(F32)
    g = (y * _silu(z_ref[0].astype(F32))).astype(BF16)
    d = jnp.dot(g, w_ref[...], preferred_element_type=F32)
    gate = mod_ref[0, :, 2 * D_MODEL:3 * D_MODEL]
    xo = xt + gate * d
    if final:
        ms = jnp.mean(xo * xo, axis=-1, keepdims=True)
        xo = xo * lax.rsqrt(ms + EPS) * fn_ref[...]
    o_ref[0] = xo


def _out_proj(x, xc, y1, y2, z, w, mod, final_gain):
    dual = xc is not None
    final = final_gain is not None
    n_tok = SEQ if final else TOK
    half = y1.shape[-1]
    if dual:
        x_specs = [
            pl.BlockSpec((1, TM, D_MODEL), lambda b, i: (b, jnp.minimum(i, N_LAT_TM - 1), 0)),
            pl.BlockSpec((1, CTX_LEN, D_MODEL), lambda b, i: (b, 0, 0)),
        ]
        xs = (x, xc)
    else:
        x_specs = [pl.BlockSpec((1, TM, D_MODEL), lambda b, i: (b, i, 0))]
        xs = (x,)
    in_specs = x_specs + [
        pl.BlockSpec((1, TM, half), lambda b, i: (b, i, 0)),
        pl.BlockSpec((1, TM, half), lambda b, i: (b, i, 0)),
        pl.BlockSpec((1, TM, 2 * half), lambda b, i: (b, i, 0)),
        pl.BlockSpec((2 * half, D_MODEL), lambda b, i: (0, 0)),
        pl.BlockSpec((1, 1, 3 * D_MODEL), lambda b, i: (jnp.where(i < N_LAT_TM, b, BATCH), 0, 0)),
    ]
    args = xs + (y1, y2, z, w, mod)
    if final:
        in_specs.append(pl.BlockSpec((1, D_MODEL), lambda b, i: (0, 0)))
        args = args + (final_gain,)
    return pl.pallas_call(
        functools.partial(_out_kernel, dual=dual, final=final),
        out_shape=jax.ShapeDtypeStruct((BATCH, n_tok, D_MODEL), F32),
        grid=(BATCH, n_tok // TM),
        in_specs=in_specs,
        out_specs=pl.BlockSpec((1, TM, D_MODEL), lambda b, i: (b, i, 0)),
        compiler_params=pltpu.CompilerParams(
            dimension_semantics=("arbitrary", "arbitrary"), vmem_limit_bytes=VMEM_LIMIT),
        name="out_proj",
    )(*args)


def _expand_heads(q_ref, lhs_scr, nq, tq):
    lane = _lane_iota(tq)
    for j in range(nq):
        qj = q_ref[0, :, j * LANES:(j + 1) * LANES].astype(F32)
        lhs_scr[(2 * j) * tq:(2 * j + 1) * tq, :] = jnp.where(lane < HALF, qj, 0.0).astype(BF16)
        lhs_scr[(2 * j + 1) * tq:(2 * j + 2) * tq, :] = jnp.where(lane >= HALF, qj, 0.0).astype(BF16)


def _scores(lhs, k):
    return lax.dot_general(lhs, k, (((1,), (1,)), ((), ())), preferred_element_type=F32)


def _merge_heads(o, tq, j):
    lane = _lane_iota(tq)
    return jnp.where(lane < HALF, o[(2 * j) * tq:(2 * j + 1) * tq], o[(2 * j + 1) * tq:(2 * j + 2) * tq])


def _dense_kernel(*refs, nq, tq, tk, n_lat_tiles, diff, lam0):
    if diff:
        q_ref, k_ref, vt_ref, lamv_ref, subln_ref, o_ref, qt_scr, m_scr, l_scr, acc_scr, s_scr = refs
    else:
        q_ref, k_ref, vt_ref, o_ref, qt_scr, m_scr, l_scr, acc_scr, s_scr = refs
    n_q = 2 * nq * tq
    feat = lax.broadcasted_iota(jnp.int32, (LANES, tq), 0)
    for j in range(nq):
        qt = q_ref[0, :, j * LANES:(j + 1) * LANES].astype(F32).T
        qt_scr[:, (2 * j) * tq:(2 * j + 1) * tq] = jnp.where(feat < HALF, qt, 0.0).astype(BF16)
        qt_scr[:, (2 * j + 1) * tq:(2 * j + 2) * tq] = jnp.where(feat >= HALF, qt, 0.0).astype(BF16)
    m_scr[...] = jnp.full((1, n_q), -jnp.inf, F32)
    l_scr[...] = jnp.zeros((1, n_q), F32)
    acc_scr[...] = jnp.zeros((LANES, n_q), F32)

    n_blk = n_q // QB
    n_sub = tk // TM
    n_chunks = SEQ // tk
    ctx_chunk = (SEQ, SEQ // TM, CTX_LEN // TM)

    def lat_chunk(c):
        start = c * tk if isinstance(c, int) else pl.multiple_of(c * tk, tk)
        return (start, c * n_sub, n_sub)

    def scores(chunk, n, half):
        start, _, subs = chunk
        kc = k_ref[0, pl.ds(start, subs * TM), :]
        s_scr[half, n, 0:subs * TM, :] = jnp.dot(kc, qt_scr[:, n * QB:(n + 1) * QB],
                                                 preferred_element_type=F32)

    def softmax_pv(chunk, n, half):
        _, tile0, subs = chunk
        cols = slice(n * QB, (n + 1) * QB)
        vts = [vt_ref[0, 0, tile0 + u] for u in range(subs)]
        vtc = vts[0] if subs == 1 else jnp.concatenate(vts, axis=1)
        s = s_scr[half, n, 0:subs * TM, :]
        m_prev = m_scr[:, cols]
        m_new = jnp.maximum(m_prev, jnp.max(s, axis=0, keepdims=True))
        alpha = jnp.exp2(m_prev - m_new)
        p = jnp.exp2(s - m_new)
        l_scr[:, cols] = alpha * l_scr[:, cols] + jnp.sum(p, axis=0, keepdims=True)
        acc_scr[:, cols] = alpha * acc_scr[:, cols] + jnp.dot(
            vtc, p.astype(BF16), preferred_element_type=F32)
        m_scr[:, cols] = m_new

    def half_step(cur, half, nxt):
        for n in range(n_blk):
            if nxt is not None:
                scores(nxt, n, 1 - half)
            softmax_pv(cur, n, half)

    def latent_keys():
        half_step(ctx_chunk, 0, lat_chunk(0))

        def body(c, carry):
            half_step(lat_chunk(2 * c), 1, lat_chunk(2 * c + 1))
            half_step(lat_chunk(2 * c + 1), 0, lat_chunk(2 * c + 2))
            return carry
        lax.fori_loop(0, n_chunks // 2 - 1, body, 0)
        half_step(lat_chunk(n_chunks - 2), 1, lat_chunk(n_chunks - 1))
        half_step(lat_chunk(n_chunks - 1), 0, None)

    def context_keys_only():
        half_step(ctx_chunk, 0, None)

    for n in range(n_blk):
        scores(ctx_chunk, n, 0)
    if n_lat_tiles is None:
        latent_keys()
    else:
        is_latent = pl.program_id(2) < n_lat_tiles
        pl.when(is_latent)(latent_keys)
        pl.when(jnp.logical_not(is_latent))(context_keys_only)

    ot = acc_scr[...] / l_scr[...]
    if diff:
        lv = lamv_ref[...]
        lam = (jnp.exp(jnp.sum(lv[0:1] * lv[1:2], axis=-1, keepdims=True))
               - jnp.exp(jnp.sum(lv[2:3] * lv[3:4], axis=-1, keepdims=True)) + lam0)
        dlt = (ot[:, 0:tq] - lam * ot[:, tq:2 * tq]).T
        ms = jnp.mean(dlt * dlt, axis=-1, keepdims=True)
        y = dlt * lax.rsqrt(ms + EPS) * subln_ref[...] * (1.0 - lam0)
        o_ref[0] = y.astype(BF16)
    else:
        pieces = []
        for j in range(nq):
            blk = jnp.where(feat < HALF, ot[:, (2 * j) * tq:(2 * j + 1) * tq],
                            ot[:, (2 * j + 1) * tq:(2 * j + 2) * tq])
            pieces.append(blk.T.astype(BF16))
        o_ref[0] = pieces[0] if nq == 1 else jnp.concatenate(pieces, axis=-1)


def _dense_attn(q, k, vt, *, nq, tq, tk, n_q_tok, diff=False, lamv=None, subln=None, lam0=0.0):
    groups = q.shape[-1] // (nq * LANES)
    n_tiles = n_q_tok // tq
    n_lat_tiles = None if n_q_tok == SEQ else SEQ // tq
    n_q = 2 * nq * tq
    in_specs = [
        pl.BlockSpec((1, tq, nq * LANES), lambda b, g, i: (b, i, g)),
        pl.BlockSpec((1, TOK, LANES), lambda b, g, i: (b, 0, g)),
        pl.BlockSpec((1, 1, TOK // TM, LANES, TM), lambda b, g, i: (b, g, 0, 0, 0)),
    ]
    args = (q, k, vt)
    if diff:
        in_specs += [pl.BlockSpec((4, HEAD_DIM), lambda b, g, i: (0, 0)),
                     pl.BlockSpec((1, LANES), lambda b, g, i: (0, 0))]
        args = args + (lamv, subln)
    return pl.pallas_call(
        functools.partial(_dense_kernel, nq=nq, tq=tq, tk=tk, n_lat_tiles=n_lat_tiles,
                          diff=diff, lam0=lam0),
        out_shape=jax.ShapeDtypeStruct((BATCH, n_q_tok, groups * nq * LANES), BF16),
        grid=(BATCH, groups, n_tiles),
        in_specs=in_specs,
        out_specs=pl.BlockSpec((1, tq, nq * LANES), lambda b, g, i: (b, i, g)),
        scratch_shapes=[pltpu.VMEM((LANES, n_q), BF16),
                        pltpu.VMEM((1, n_q), F32),
                        pltpu.VMEM((1, n_q), F32),
                        pltpu.VMEM((LANES, n_q), F32),
                        pltpu.VMEM((2, n_q // QB, tk, QB), F32)],
        compiler_params=pltpu.CompilerParams(
            dimension_semantics=("arbitrary", "arbitrary", "arbitrary"),
            vmem_limit_bytes=VMEM_LIMIT),
        name="dense_diff_attn" if diff else "dense_gqa_attn",
    )(*args)


def _nb_kernel(*refs):
    q_ref = refs[0]
    k_refs = refs[1:1 + NB_BLKS]
    v_refs = refs[1 + NB_BLKS:1 + 2 * NB_BLKS]
    kc_ref, vc_ref, bias_ref, o_ref, lhs_scr = refs[1 + 2 * NB_BLKS:]
    tq = TQ_BLK
    nq = B_HEADS // 2
    _expand_heads(q_ref, lhs_scr, nq, tq)
    pieces = []
    for j in range(nq):
        cs = slice(j * LANES, (j + 1) * LANES)
        lhs = lhs_scr[(2 * j) * tq:(2 * j + 2) * tq, :]
        kcat = jnp.concatenate([r[0, :, cs] for r in k_refs], axis=0)
        vcat = jnp.concatenate([r[0, :, cs] for r in v_refs], axis=0)
        s_n = _scores(lhs, kcat) + bias_ref[0, j]
        s_c = _scores(lhs, kc_ref[0, :, cs])
        m = jnp.maximum(jnp.max(s_n, axis=-1, keepdims=True), jnp.max(s_c, axis=-1, keepdims=True))
        p_n = jnp.exp2(s_n - m)
        p_c = jnp.exp2(s_c - m)
        l = jnp.sum(p_n, axis=-1, keepdims=True) + jnp.sum(p_c, axis=-1, keepdims=True)
        o = (jnp.dot(p_n.astype(BF16), vcat, preferred_element_type=F32)
             + jnp.dot(p_c.astype(BF16), vc_ref[0, :, cs], preferred_element_type=F32)) / l
        pieces.append(_merge_heads(o, tq, 0).astype(BF16))
    o_ref[0] = jnp.concatenate(pieces, axis=-1)


def _nb_case(i):
    last = N_LAT_TQ - 1
    return jnp.where(i == 0, 0, jnp.where(i == 1, 1, jnp.where(
        i == last - 1, 3, jnp.where(i == last, 4, jnp.where(i > last, 5, 2)))))


def _nb_attn(q, k, v, bias):
    width = B_HEADS * HEAD_DIM
    n_tiles = N_LAT_TQ + N_CTX_TQ

    def kv_spec(t):
        return pl.BlockSpec((1, TQ_BLK, width),
                            lambda b, i: (b, jnp.clip(i - 2, 0, N_LAT_TQ - NB_BLKS) + t, 0))

    in_specs = ([pl.BlockSpec((1, TQ_BLK, width), lambda b, i: (b, i, 0))]
                + [kv_spec(t) for t in range(NB_BLKS)]
                + [kv_spec(t) for t in range(NB_BLKS)]
                + [pl.BlockSpec((1, CTX_LEN, width), lambda b, i: (b, SEQ // CTX_LEN, 0)),
                   pl.BlockSpec((1, CTX_LEN, width), lambda b, i: (b, SEQ // CTX_LEN, 0)),
                   pl.BlockSpec((1, B_HEADS // 2, 2 * TQ_BLK, NB_KEYS),
                                lambda b, i: (_nb_case(i), 0, 0, 0))])
    return pl.pallas_call(
        _nb_kernel,
        out_shape=jax.ShapeDtypeStruct((BATCH, TOK, width), BF16),
        grid=(BATCH, n_tiles),
        in_specs=in_specs,
        out_specs=pl.BlockSpec((1, TQ_BLK, width), lambda b, i: (b, i, 0)),
        scratch_shapes=[pltpu.VMEM((B_HEADS * TQ_BLK, LANES), BF16)],
        compiler_params=pltpu.CompilerParams(
            dimension_semantics=("arbitrary", "arbitrary"), vmem_limit_bytes=VMEM_LIMIT),
        name="neighbourhood_attn",
    )(q, *([k] * NB_BLKS), *([v] * NB_BLKS), k, v, bias)


def _nb_bias_table(rpb):
    grid_spec = pltpu.PrefetchScalarGridSpec(
        num_scalar_prefetch=1,
        grid=(6, B_HEADS),
        in_specs=[],
        out_specs=pl.BlockSpec((1, 1, TQ_BLK, NB_KEYS), lambda c, h, r: (c, h, 0, 0)),
    )
    tab = pl.pallas_call(
        _nb_bias_kernel,
        out_shape=jax.ShapeDtypeStruct((6, B_HEADS, TQ_BLK, NB_KEYS), F32),
        grid_spec=grid_spec,
        compiler_params=pltpu.CompilerParams(dimension_semantics=("arbitrary", "arbitrary")),
        name="nb_bias_table",
    )(rpb.astype(F32).reshape(-1))
    return tab.reshape(6, B_HEADS // 2, 2 * TQ_BLK, NB_KEYS)


def _nb_bias_kernel(rpb_ref, o_ref):
    c = pl.program_id(0)
    h = pl.program_id(1)
    last = N_LAT_TQ - 1
    tile = jnp.where(c == 3, last - 1, jnp.where(c == 4, last, jnp.minimum(c, 2)))
    r0 = 2 * tile
    base = jnp.clip(2 * tile - 4, 0, GRID_ROWS - NB_ROWS)
    n_dr = 2 * NA_ROWS - 1
    n_dc = 2 * NA_COLS - 1
    qc = lax.broadcasted_iota(jnp.int32, (GRID_W, LANES), 0)
    lane = lax.broadcasted_iota(jnp.int32, (GRID_W, LANES), 1)
    kc = lane & (GRID_W - 1)
    second = lane >= GRID_W
    dc = kc - qc + (NA_COLS - 1)
    cs = jnp.clip(qc - NA_COLS // 2, 0, GRID_W - NA_COLS)
    col_ok = jnp.where(kc >= cs, jnp.where(kc < cs + NA_COLS, 1, 0), 0)
    for qr in range(TQ_BLK // GRID_W):
        r = r0 + qr
        rs = jnp.clip(r - NA_ROWS // 2, 0, GRID_ROWS - NA_ROWS)
        for t in range(NB_ROWS // 2):
            kr = base + 2 * t
            dr = kr - r + (NA_ROWS - 1)
            ok0 = jnp.where((kr >= rs) & (kr < rs + NA_ROWS) & (c < 5), 1, 0)
            ok1 = jnp.where((kr + 1 >= rs) & (kr + 1 < rs + NA_ROWS) & (c < 5), 1, 0)
            off0 = (h * n_dr + jnp.clip(dr, 0, n_dr - 1)) * n_dc
            off1 = (h * n_dr + jnp.clip(dr + 1, 0, n_dr - 1)) * n_dc
            acc = jnp.zeros((GRID_W, LANES), F32)
            for j in range(n_dc):
                val = jnp.where(second, rpb_ref[off1 + j], rpb_ref[off0 + j])
                acc = jnp.where(dc == j, val, acc)
            ok = col_ok * jnp.where(second, ok1, ok0)
            o_ref[0, 0, qr * GRID_W:(qr + 1) * GRID_W, t * LANES:(t + 1) * LANES] = jnp.where(
                ok > 0, acc * LOG2E, NEG_INF)


def _win_kernel(sink_ref, q_ref, kp_ref, kq_ref, kn_ref, vp_ref, vq_ref, vn_ref, kc_ref, vc_ref,
                o_ref, lhs_scr):
    tq = TQ_BLK
    nq = D_HEADS // 2
    i = pl.program_id(1)
    _expand_heads(q_ref, lhs_scr, nq, tq)
    kcat = jnp.concatenate([kp_ref[0], kq_ref[0], kn_ref[0], kc_ref[0]], axis=0)
    vcat = jnp.concatenate([vp_ref[0], vq_ref[0], vn_ref[0], vc_ref[0]], axis=0)
    span = 3 * tq
    n_keys = span + CTX_LEN
    rr = lax.broadcasted_iota(jnp.int32, (tq, n_keys), 0)
    cc = lax.broadcasted_iota(jnp.int32, (tq, n_keys), 1)
    rel = cc - tq - rr
    kpos = i * tq - tq + cc
    valid = (cc >= span) | ((jnp.abs(rel) <= D_WINDOW) & (kpos >= 0) & (kpos < SEQ))
    s = _scores(lhs_scr[...], kcat)
    outs = []
    for g in range(2 * nq):
        head = HEAD_PERM[g]
        sink = sink_ref[head]
        sg = jnp.where(valid, s[g * tq:(g + 1) * tq], NEG_INF)
        m = jnp.maximum(jnp.max(sg, axis=-1, keepdims=True), sink)
        p = jnp.exp2(sg - m)
        l = jnp.sum(p, axis=-1, keepdims=True) + jnp.exp2(sink - m)
        outs.append(jnp.dot(p.astype(BF16), vcat, preferred_element_type=F32) / l)
    o = jnp.concatenate(outs, axis=0)
    o_ref[0] = jnp.concatenate([_merge_heads(o, tq, j).astype(BF16) for j in range(nq)], axis=-1)


def _win_attn(q, k, v, sinks):
    width = D_HEADS * HEAD_DIM
    last = N_LAT_TQ - 1

    def kv_spec(off):
        return pl.BlockSpec((1, TQ_BLK, LANES), lambda b, i, s: (b, jnp.clip(i + off, 0, last), 0))

    ctx_spec = pl.BlockSpec((1, CTX_LEN, LANES), lambda b, i, s: (b, SEQ // CTX_LEN, 0))
    grid_spec = pltpu.PrefetchScalarGridSpec(
        num_scalar_prefetch=1,
        grid=(BATCH, N_LAT_TQ),
        in_specs=[pl.BlockSpec((1, TQ_BLK, width), lambda b, i, s: (b, i, 0)),
                  kv_spec(-1), kv_spec(0), kv_spec(1),
                  kv_spec(-1), kv_spec(0), kv_spec(1),
                  ctx_spec, ctx_spec],
        out_specs=pl.BlockSpec((1, TQ_BLK, width), lambda b, i, s: (b, i, 0)),
        scratch_shapes=[pltpu.VMEM((D_HEADS * TQ_BLK, LANES), BF16)],
    )
    return pl.pallas_call(
        _win_kernel,
        out_shape=jax.ShapeDtypeStruct((BATCH, SEQ, width), BF16),
        grid_spec=grid_spec,
        compiler_params=pltpu.CompilerParams(
            dimension_semantics=("arbitrary", "arbitrary"), vmem_limit_bytes=VMEM_LIMIT),
        name="window_attn",
    )(sinks, q, k, k, k, v, v, v, k, v)


def _rope_tables():
    t = jnp.arange(SEQ, dtype=jnp.int32)
    row = (t // GRID_W).astype(F32)
    col = (t % GRID_W).astype(F32)
    quarter = HEAD_DIM // 4
    inv_freq = ROPE_THETA ** (-jnp.arange(quarter, dtype=F32) / quarter)
    ang_r = row[:, None] * inv_freq
    ang_c = col[:, None] * inv_freq
    cr, sr, cc, sc = jnp.cos(ang_r), jnp.sin(ang_r), jnp.cos(ang_c), jnp.sin(ang_c)
    cos = jnp.tile(jnp.concatenate([cr, cr, cc, cc], axis=-1), (1, LANES // HEAD_DIM))
    sin = jnp.tile(jnp.concatenate([-sr, sr, -sc, sc], axis=-1), (1, LANES // HEAD_DIM))
    cos = jnp.concatenate([cos, jnp.ones((CTX_LEN, LANES), F32)], axis=0)
    sin = jnp.concatenate([sin, jnp.zeros((CTX_LEN, LANES), F32)], axis=0)
    return cos, sin


def _head_cols(perm):
    return np.concatenate([np.arange(HEAD_DIM) + HEAD_DIM * h for h in perm])


def kernel(x, c, ctx, c_ctx, w_mod, b_mod, w_in_even, w_out_even, a_q_norm, a_k_norm, b_rpb,
           w_in_odd, w_out_odd, c_lambda, c_subln, d_sinks, final_norm):
    assert DEPTH == 2 and x.shape == (BATCH, SEQ, D_MODEL) and ctx.shape == (BATCH, CTX_LEN, D_MODEL)
    perm = _head_cols(HEAD_PERM)
    hw = A_HEADS * HEAD_DIM

    cs = jnp.concatenate([c, c_ctx[None], jnp.zeros((8 - BATCH - 1, D_MODEL), F32)], axis=0)
    mod = _modulation(cs, w_mod, b_mod)
    mod0 = mod[0].reshape(8, 1, 3 * D_MODEL)
    mod1 = mod[1].reshape(8, 1, 3 * D_MODEL)

    cos, sin = _rope_tables()
    seg = jnp.asarray(np.kron(np.eye(LANES // HEAD_DIM), np.ones((HEAD_DIM, HEAD_DIM))), BF16)
    tile2 = LANES // HEAD_DIM
    qg = jnp.tile(a_q_norm[0].astype(F32), tile2).reshape(1, LANES)
    kg = jnp.tile(a_k_norm[0].astype(F32), tile2).reshape(1, LANES)

    z0 = 2304
    cols0 = np.concatenate([perm, np.arange(hw, z0), z0 + perm, np.arange(z0 + hw, z0 + 2 * hw)])
    w_in0 = w_in_even[0][:, cols0].astype(BF16)
    w_out0 = w_out_even[0][np.concatenate([perm, np.arange(hw, 2 * hw)])].astype(BF16)
    plan0 = (
        (hw, ((0, 0, hw, "q", True, True, False),)),
        (2 * LANES, ((1, 0, LANES, "k", True, False, False),
                     (2, LANES, 2 * LANES, None, False, False, True))),
        (hw, ((3, 0, hw, None, False, True, False),)),
        (hw, ((4, 0, hw, None, False, False, False),)),
        (hw, ((5, 0, hw, None, False, False, False),)),
        (2 * hw, ((6, 0, 2 * hw, None, False, False, False),)),
    )
    qa, ka, va, qb, kb, vb, zz = _in_proj(x, ctx, mod0, w_in0, qg, kg, cos, sin, seg, plan0,
                                          (hw, LANES, LANES, hw, hw, hw, 2 * hw))
    ya = _dense_attn(qa, ka, va, nq=4, tq=128, tk=512, n_q_tok=TOK)
    yb = _nb_attn(qb, kb, vb, _nb_bias_table(b_rpb[0]))
    x1 = _out_proj(x, ctx, ya, yb, zz, w_out0, mod0, None)

    cols1 = np.concatenate([np.arange(0, 3 * hw), 3 * hw + perm, np.arange(4 * hw, z0 + hw), z0 + hw + perm])
    w_in1 = w_in_odd[0][:, cols1].astype(BF16)
    w_out1 = w_out_odd[0][np.concatenate([np.arange(hw), hw + perm])].astype(BF16)
    plan1 = (
        (hw, ((0, 0, hw, None, True, True, False),)),
        (hw, ((1, 0, hw, None, True, False, False),)),
        (hw, ((2, 0, hw, None, False, False, True),)),
        (hw, ((3, 0, hw, None, True, True, False),)),
        (2 * LANES, ((4, 0, LANES, None, True, False, False),
                     (5, LANES, 2 * LANES, None, False, False, False))),
        (2 * hw, ((6, 0, 2 * hw, None, False, False, False),)),
    )
    qc, kc, vc, qd, kd, vd, zz1 = _in_proj(x1, None, mod1, w_in1, qg, kg, cos, sin, seg, plan1,
                                           (hw, hw, hw, hw, LANES, LANES, 2 * hw))
    yc = _dense_attn(qc, kc, vc, nq=1, tq=512, tk=512, n_q_tok=SEQ, diff=True,
                     lamv=c_lambda[0].astype(F32), subln=c_subln[0].astype(F32).reshape(1, LANES),
                     lam0=_lambda_init(1))
    yd = _win_attn(qd, kd, vd, d_sinks[0].astype(F32) * LOG2E)
    return _out_proj(x1, None, yc, yd, zz1, w_out1, mod1, final_norm.astype(F32).reshape(1, D_MODEL))
```

```python
import functools
import math

import numpy as np
import jax
import jax.numpy as jnp
from jax import lax
from jax.experimental import pallas as pl
from jax.experimental.pallas import tpu as pltpu

D_MODEL = 1024
BATCH = 4
SEQ = 8192
DEPTH = 2
GRID_W = 64
CTX_LEN = 256
HEAD_DIM = 64
ROPE_THETA = 10000.0
EPS = 1e-6
NEG_INF = -1e30
A_HEADS = 8
B_HEADS = 8
NA_ROWS = 8
NA_COLS = 16
C_HEADS = 4
D_HEADS = 8
D_WINDOW = 128
SCALE = HEAD_DIM ** -0.5
LOG2E = math.log2(math.e)
QSCALE = SCALE * LOG2E

LANES = 128
HALF = HEAD_DIM
TOK = SEQ + CTX_LEN
TM = 256
N_LAT_TM = SEQ // TM
QB = 256
DENSE_TK = 1024
TQ_BLK = 128
N_LAT_TQ = SEQ // TQ_BLK
N_CTX_TQ = CTX_LEN // TQ_BLK
GRID_ROWS = SEQ // GRID_W
NB_ROWS = 10
NB_KEYS = NB_ROWS * GRID_W
NB_BLKS = NB_KEYS // TQ_BLK
VMEM_LIMIT = 56 * 1024 * 1024

HEAD_PERM = (0, 4, 1, 5, 2, 6, 3, 7)
F32 = jnp.float32
BF16 = jnp.bfloat16


def _lambda_init(layer):
    return 0.8 - 0.6 * math.exp(-0.3 * layer)


def _lane_iota(rows):
    return lax.broadcasted_iota(jnp.int32, (rows, LANES), 1)


def _silu(z):
    return z / (1.0 + jnp.exp(-z))


def _mod_kernel(c_ref, w_ref, b_ref, o_ref):
    o_ref[0] = jnp.dot(_silu(c_ref[...]), w_ref[0], preferred_element_type=F32) + b_ref[0]


def _modulation(cs, w_mod, b_mod):
    tn = D_MODEL
    return pl.pallas_call(
        _mod_kernel,
        out_shape=jax.ShapeDtypeStruct((DEPTH, 8, 3 * D_MODEL), F32),
        grid=(DEPTH, 3 * D_MODEL // tn),
        in_specs=[
            pl.BlockSpec((8, D_MODEL), lambda l, j: (0, 0)),
            pl.BlockSpec((1, D_MODEL, tn), lambda l, j: (l, 0, j)),
            pl.BlockSpec((1, 1, tn), lambda l, j: (l, 0, j)),
        ],
        out_specs=pl.BlockSpec((1, 8, tn), lambda l, j: (l, 0, j)),
        compiler_params=pltpu.CompilerParams(
            dimension_semantics=("arbitrary", "arbitrary"), vmem_limit_bytes=VMEM_LIMIT),
        name="modulation",
    )(cs, w_mod, b_mod.reshape(DEPTH, 1, 3 * D_MODEL))


def _swap16(x):
    lane = _lane_iota(x.shape[0])
    fwd = pltpu.roll(x, LANES - 16, 1)
    bwd = pltpu.roll(x, 16, 1)
    return jnp.where((lane & 16) == 0, fwd, bwd)


def _proj_kernel(*refs, plan, dual):
    if dual:
        x_ref, xc_ref = refs[:2]
        refs = refs[2:]
    else:
        x_ref = refs[0]
        refs = refs[1:]
    mod_ref, w_ref, qg_ref, kg_ref, cos_ref, sin_ref, seg_ref = refs[:7]
    out_refs = refs[7:]

    xt = x_ref[0]
    if dual:
        xt = jnp.where(pl.program_id(1) < N_LAT_TM, xt, xc_ref[0])
    ms = jnp.mean(xt * xt, axis=-1, keepdims=True)
    xn = xt * lax.rsqrt(ms + EPS)
    shift = mod_ref[0, :, 0:D_MODEL]
    scale = mod_ref[0, :, D_MODEL:2 * D_MODEL]
    h = (xn * (1.0 + scale) + shift).astype(BF16)

    cos = cos_ref[...]
    sin = sin_ref[...]
    seg = seg_ref[...]
    col = 0
    for width, outs in plan:
        acc = jnp.dot(h, w_ref[:, col:col + width], preferred_element_type=F32)
        col += width
        for out_idx, lo, hi, norm, rope, scale_q, transpose in outs:
            pieces = []
            for c0 in range(lo, hi, LANES):
                blk = acc[:, c0:c0 + LANES]
                if transpose:
                    out_refs[out_idx][0, (c0 - lo) // LANES, 0] = blk.T.astype(BF16)
                    continue
                if norm:
                    sq = blk * blk
                    sq_hi = sq.astype(BF16)
                    sq_lo = (sq - sq_hi.astype(F32)).astype(BF16)
                    ssum = (jnp.dot(sq_hi, seg, preferred_element_type=F32)
                            + jnp.dot(sq_lo, seg, preferred_element_type=F32))
                    gain = qg_ref[...] if norm == "q" else kg_ref[...]
                    blk = blk * lax.rsqrt(ssum * (1.0 / HEAD_DIM) + EPS) * gain
                if rope:
                    blk = blk * cos + _swap16(blk) * sin
                if scale_q:
                    blk = blk * QSCALE
                pieces.append(blk.astype(BF16))
            if pieces:
                out_refs[out_idx][0] = pieces[0] if len(pieces) == 1 else jnp.concatenate(pieces, axis=-1)


def _in_proj(x, xc, mod, w, qg, kg, cos, sin, seg, plan, out_widths):
    dual = xc is not None
    n_tiles = TOK // TM
    if dual:
        x_specs = [
            pl.BlockSpec((1, TM, D_MODEL), lambda b, i: (b, jnp.minimum(i, N_LAT_TM - 1), 0)),
            pl.BlockSpec((1, CTX_LEN, D_MODEL), lambda b, i: (b, 0, 0)),
        ]
        xs = (x, xc)
    else:
        x_specs = [pl.BlockSpec((1, TM, D_MODEL), lambda b, i: (b, i, 0))]
        xs = (x,)
    n_in = w.shape[1]
    in_specs = x_specs + [
        pl.BlockSpec((1, 1, 3 * D_MODEL), lambda b, i: (jnp.where(i < N_LAT_TM, b, BATCH), 0, 0)),
        pl.BlockSpec((D_MODEL, n_in), lambda b, i: (0, 0)),
        pl.BlockSpec((1, LANES), lambda b, i: (0, 0)),
        pl.BlockSpec((1, LANES), lambda b, i: (0, 0)),
        pl.BlockSpec((TM, LANES), lambda b, i: (i, 0)),
        pl.BlockSpec((TM, LANES), lambda b, i: (i, 0)),
        pl.BlockSpec((LANES, LANES), lambda b, i: (0, 0)),
    ]
    transposed = {o[0] for _, outs in plan for o in outs if o[6]}
    out_shape, out_specs = [], []
    for idx, wd in enumerate(out_widths):
        if idx in transposed:
            out_shape.append(jax.ShapeDtypeStruct((BATCH, wd // LANES, n_tiles, LANES, TM), BF16))
            out_specs.append(pl.BlockSpec((1, wd // LANES, 1, LANES, TM), lambda b, i: (b, 0, i, 0, 0)))
        else:
            out_shape.append(jax.ShapeDtypeStruct((BATCH, TOK, wd), BF16))
            out_specs.append(pl.BlockSpec((1, TM, wd), lambda b, i: (b, i, 0)))
    return pl.pallas_call(
        functools.partial(_proj_kernel, plan=plan, dual=dual),
        out_shape=tuple(out_shape),
        grid=(BATCH, n_tiles),
        in_specs=in_specs,
        out_specs=tuple(out_specs),
        compiler_params=pltpu.CompilerParams(
            dimension_semantics=("arbitrary", "arbitrary"), vmem_limit_bytes=VMEM_LIMIT),
        name="in_proj",
    )(*xs, mod, w, qg, kg, cos, sin, seg)


def _out_kernel(*refs, dual, final):
    if dual:
        x_ref, xc_ref = refs[:2]
        refs = refs[2:]
    else:
        x_ref = refs[0]
        refs = refs[1:]
    y1_ref, y2_ref, z_ref, w_ref, mod_ref = refs[:5]
    refs = refs[5:]
    if final:
        fn_ref, o_ref = refs
    else:
        (o_ref,) = refs

    xt = x_ref[0]
    if dual:
        xt = jnp.where(pl.program_id(1) < N_LAT_TM, xt, xc_ref[0])
    y = jnp.concatenate([y1_ref[0], y2_ref[0]], axis=-1).astype(F32)
    g = (y * _silu(z_ref[0].astype(F32))).astype(BF16)
    d = jnp.dot(g, w_ref[...], preferred_element_type=F32)
    gate = mod_ref[0, :, 2 * D_MODEL:3 * D_MODEL]
    xo = xt + gate * d
    if final:
        ms = jnp.mean(xo * xo, axis=-1, keepdims=True)
        xo = xo * lax.rsqrt(ms + EPS) * fn_ref[...]
    o_ref[0] = xo


def _out_proj(x, xc, y1, y2, z, w, mod, final_gain):
    dual = xc is not None
    final = final_gain is not None
    n_tok = SEQ if final else TOK
    half = y1.shape[-1]
    if dual:
        x_specs = [
            pl.BlockSpec((1, TM, D_MODEL), lambda b, i: (b, jnp.minimum(i, N_LAT_TM - 1), 0)),
            pl.BlockSpec((1, CTX_LEN, D_MODEL), lambda b, i: (b, 0, 0)),
        ]
        xs = (x, xc)
    else:
        x_specs = [pl.BlockSpec((1, TM, D_MODEL), lambda b, i: (b, i, 0))]
        xs = (x,)
    in_specs = x_specs + [
        pl.BlockSpec((1, TM, half), lambda b, i: (b, i, 0)),
        pl.BlockSpec((1, TM, half), lambda b, i: (b, i, 0)),
        pl.BlockSpec((1, TM, 2 * half), lambda b, i: (b, i, 0)),
        pl.BlockSpec((2 * half, D_MODEL), lambda b, i: (0, 0)),
        pl.BlockSpec((1, 1, 3 * D_MODEL), lambda b, i: (jnp.where(i < N_LAT_TM, b, BATCH), 0, 0)),
    ]
    args = xs + (y1, y2, z, w, mod)
    if final:
        in_specs.append(pl.BlockSpec((1, D_MODEL), lambda b, i: (0, 0)))
        args = args + (final_gain,)
    return pl.pallas_call(
        functools.partial(_out_kernel, dual=dual, final=final),
        out_shape=jax.ShapeDtypeStruct((BATCH, n_tok, D_MODEL), F32),
        grid=(BATCH, n_tok // TM),
        in_specs=in_specs,
        out_specs=pl.BlockSpec((1, TM, D_MODEL), lambda b, i: (b, i, 0)),
        compiler_params=pltpu.CompilerParams(
            dimension_semantics=("arbitrary", "arbitrary"), vmem_limit_bytes=VMEM_LIMIT),
        name="out_proj",
    )(*args)


def _expand_heads(q_ref, lhs_scr, nq, tq):
    lane = _lane_iota(tq)
    for j in range(nq):
        qj = q_ref[0, :, j * LANES:(j + 1) * LANES].astype(F32)
        lhs_scr[(2 * j) * tq:(2 * j + 1) * tq, :] = jnp.where(lane < HALF, qj, 0.0).astype(BF16)
        lhs_scr[(2 * j + 1) * tq:(2 * j + 2) * tq, :] = jnp.where(lane >= HALF, qj, 0.0).astype(BF16)


def _scores(lhs, k):
    return lax.dot_general(lhs, k, (((1,), (1,)), ((), ())), preferred_element_type=F32)


def _merge_heads(o, tq, j):
    lane = _lane_iota(tq)
    return jnp.where(lane < HALF, o[(2 * j) * tq:(2 * j + 1) * tq], o[(2 * j + 1) * tq:(2 * j + 2) * tq])


def _dense_kernel(*refs, nq, tq, tk, n_lat_tiles, diff, lam0):
    if diff:
        q_ref, k_ref, vt_ref, lamv_ref, subln_ref, o_ref, qt_scr, m_scr, l_scr, acc_scr, s_scr = refs
    else:
        q_ref, k_ref, vt_ref, o_ref, qt_scr, m_scr, l_scr, acc_scr, s_scr = refs
    n_q = 2 * nq * tq
    feat = lax.broadcasted_iota(jnp.int32, (LANES, tq), 0)
    for j in range(nq):
        qt = q_ref[0, :, j * LANES:(j + 1) * LANES].astype(F32).T
        qt_scr[:, (2 * j) * tq:(2 * j + 1) * tq] = jnp.where(feat < HALF, qt, 0.0).astype(BF16)
        qt_scr[:, (2 * j + 1) * tq:(2 * j + 2) * tq] = jnp.where(feat >= HALF, qt, 0.0).astype(BF16)
    m_scr[...] = jnp.full((1, n_q), -jnp.inf, F32)
    l_scr[...] = jnp.zeros((1, n_q), F32)
    acc_scr[...] = jnp.zeros((LANES, n_q), F32)

    n_blk = n_q // QB
    n_sub = tk // TM
    n_chunks = SEQ // tk
    ctx_chunk = (SEQ, SEQ // TM, CTX_LEN // TM)

    def lat_chunk(c):
        start = c * tk if isinstance(c, int) else pl.multiple_of(c * tk, tk)
        return (start, c * n_sub, n_sub)

    def scores(chunk, n, half):
        start, _, subs = chunk
        kc = k_ref[0, pl.ds(start, subs * TM), :]
        s_scr[half, n, 0:subs * TM, :] = jnp.dot(kc, qt_scr[:, n * QB:(n + 1) * QB],
                                                 preferred_element_type=F32)

    def softmax_pv(chunk, n, half):
        _, tile0, subs = chunk
        cols = slice(n * QB, (n + 1) * QB)
        vts = [vt_ref[0, 0, tile0 + u] for u in range(subs)]
        vtc = vts[0] if subs == 1 else jnp.concatenate(vts, axis=1)
        s = s_scr[half, n, 0:subs * TM, :]
        m_prev = m_scr[:, cols]
        m_new = jnp.maximum(m_prev, jnp.max(s, axis=0, keepdims=True))
        alpha = jnp.exp2(m_prev - m_new)
        p = jnp.exp2(s - m_new)
        l_scr[:, cols] = alpha * l_scr[:, cols] + jnp.sum(p, axis=0, keepdims=True)
        acc_scr[:, cols] = alpha * acc_scr[:, cols] + jnp.dot(
            vtc, p.astype(BF16), preferred_element_type=F32)
        m_scr[:, cols] = m_new

    def half_step(cur, half, nxt):
        for n in range(n_blk):
            if nxt is not None:
                scores(nxt, n, 1 - half)
            softmax_pv(cur, n, half)

    def latent_keys():
        half_step(ctx_chunk, 0, lat_chunk(0))
        for c in range(n_chunks):
            half_step(lat_chunk(c), (c + 1) % 2, lat_chunk(c + 1) if c + 1 < n_chunks else None)

    def context_keys_only():
        half_step(ctx_chunk, 0, None)

    for n in range(n_blk):
        scores(ctx_chunk, n, 0)
    is_latent = pl.program_id(2) < n_lat_tiles
    pl.when(is_latent)(latent_keys)
    pl.when(jnp.logical_not(is_latent))(context_keys_only)

    ot = acc_scr[...] / l_scr[...]
    if diff:
        lv = lamv_ref[...]
        lam = (jnp.exp(jnp.sum(lv[0:1] * lv[1:2], axis=-1, keepdims=True))
               - jnp.exp(jnp.sum(lv[2:3] * lv[3:4], axis=-1, keepdims=True)) + lam0)
        dlt = (ot[:, 0:tq] - lam * ot[:, tq:2 * tq]).T
        ms = jnp.mean(dlt * dlt, axis=-1, keepdims=True)
        y = dlt * lax.rsqrt(ms + EPS) * subln_ref[...] * (1.0 - lam0)
        o_ref[0] = y.astype(BF16)
    else:
        pieces = []
        for j in range(nq):
            blk = jnp.where(feat < HALF, ot[:, (2 * j) * tq:(2 * j + 1) * tq],
                            ot[:, (2 * j + 1) * tq:(2 * j + 2) * tq])
            pieces.append(blk.T.astype(BF16))
        o_ref[0] = pieces[0] if nq == 1 else jnp.concatenate(pieces, axis=-1)


def _dense_attn(q, k, vt, *, nq, tq, tk, n_q_tok, diff=False, lamv=None, subln=None, lam0=0.0):
    groups = q.shape[-1] // (nq * LANES)
    n_tiles = n_q_tok // tq
    n_lat_tiles = SEQ // tq
    n_q = 2 * nq * tq
    in_specs = [
        pl.BlockSpec((1, tq, nq * LANES), lambda b, g, i: (b, i, g)),
        pl.BlockSpec((1, TOK, LANES), lambda b, g, i: (b, 0, g)),
        pl.BlockSpec((1, 1, TOK // TM, LANES, TM), lambda b, g, i: (b, g, 0, 0, 0)),
    ]
    args = (q, k, vt)
    if diff:
        in_specs += [pl.BlockSpec((4, HEAD_DIM), lambda b, g, i: (0, 0)),
                     pl.BlockSpec((1, LANES), lambda b, g, i: (0, 0))]
        args = args + (lamv, subln)
    return pl.pallas_call(
        functools.partial(_dense_kernel, nq=nq, tq=tq, tk=tk, n_lat_tiles=n_lat_tiles,
                          diff=diff, lam0=lam0),
        out_shape=jax.ShapeDtypeStruct((BATCH, n_q_tok, groups * nq * LANES), BF16),
        grid=(BATCH, groups, n_tiles),
        in_specs=in_specs,
        out_specs=pl.BlockSpec((1, tq, nq * LANES), lambda b, g, i: (b, i, g)),
        scratch_shapes=[pltpu.VMEM((LANES, n_q), BF16),
                        pltpu.VMEM((1, n_q), F32),
                        pltpu.VMEM((1, n_q), F32),
                        pltpu.VMEM((LANES, n_q), F32),
                        pltpu.VMEM((2, n_q // QB, tk, QB), F32)],
        compiler_params=pltpu.CompilerParams(
            dimension_semantics=("arbitrary", "arbitrary", "arbitrary"),
            vmem_limit_bytes=VMEM_LIMIT),
        name="dense_diff_attn" if diff else "dense_gqa_attn",
    )(*args)


def _nb_kernel(*refs):
    q_ref = refs[0]
    k_refs = refs[1:1 + NB_BLKS]
    v_refs = refs[1 + NB_BLKS:1 + 2 * NB_BLKS]
    kc_ref, vc_ref, bias_ref, o_ref, lhs_scr = refs[1 + 2 * NB_BLKS:]
    tq = TQ_BLK
    nq = B_HEADS // 2
    _expand_heads(q_ref, lhs_scr, nq, tq)
    pieces = []
    for j in range(nq):
        cs = slice(j * LANES, (j + 1) * LANES)
        lhs = lhs_scr[(2 * j) * tq:(2 * j + 2) * tq, :]
        kcat = jnp.concatenate([r[0, :, cs] for r in k_refs], axis=0)
        vcat = jnp.concatenate([r[0, :, cs] for r in v_refs], axis=0)
        s_n = _scores(lhs, kcat) + bias_ref[0, j]
        s_c = _scores(lhs, kc_ref[0, :, cs])
        m = jnp.maximum(jnp.max(s_n, axis=-1, keepdims=True), jnp.max(s_c, axis=-1, keepdims=True))
        p_n = jnp.exp2(s_n - m)
        p_c = jnp.exp2(s_c - m)
        l = jnp.sum(p_n, axis=-1, keepdims=True) + jnp.sum(p_c, axis=-1, keepdims=True)
        o = (jnp.dot(p_n.astype(BF16), vcat, preferred_element_type=F32)
             + jnp.dot(p_c.astype(BF16), vc_ref[0, :, cs], preferred_element_type=F32)) / l
        pieces.append(_merge_heads(o, tq, 0).astype(BF16))
    o_ref[0] = jnp.concatenate(pieces, axis=-1)


def _nb_case(i):
    last = N_LAT_TQ - 1
    return jnp.where(i == 0, 0, jnp.where(i == 1, 1, jnp.where(
        i == last - 1, 3, jnp.where(i == last, 4, jnp.where(i > last, 5, 2)))))


def _nb_attn(q, k, v, bias):
    width = B_HEADS * HEAD_DIM
    n_tiles = N_LAT_TQ + N_CTX_TQ

    def kv_spec(t):
        return pl.BlockSpec((1, TQ_BLK, width),
                            lambda b, i: (b, jnp.clip(i - 2, 0, N_LAT_TQ - NB_BLKS) + t, 0))

    in_specs = ([pl.BlockSpec((1, TQ_BLK, width), lambda b, i: (b, i, 0))]
                + [kv_spec(t) for t in range(NB_BLKS)]
                + [kv_spec(t) for t in range(NB_BLKS)]
                + [pl.BlockSpec((1, CTX_LEN, width), lambda b, i: (b, SEQ // CTX_LEN, 0)),
                   pl.BlockSpec((1, CTX_LEN, width), lambda b, i: (b, SEQ // CTX_LEN, 0)),
                   pl.BlockSpec((1, B_HEADS // 2, 2 * TQ_BLK, NB_KEYS),
                                lambda b, i: (_nb_case(i), 0, 0, 0))])
    return pl.pallas_call(
        _nb_kernel,
        out_shape=jax.ShapeDtypeStruct((BATCH, TOK, width), BF16),
        grid=(BATCH, n_tiles),
        in_specs=in_specs,
        out_specs=pl.BlockSpec((1, TQ_BLK, width), lambda b, i: (b, i, 0)),
        scratch_shapes=[pltpu.VMEM((B_HEADS * TQ_BLK, LANES), BF16)],
        compiler_params=pltpu.CompilerParams(
            dimension_semantics=("arbitrary", "arbitrary"), vmem_limit_bytes=VMEM_LIMIT),
        name="neighbourhood_attn",
    )(q, *([k] * NB_BLKS), *([v] * NB_BLKS), k, v, bias)


def _nb_bias_table(rpb):
    grid_spec = pltpu.PrefetchScalarGridSpec(
        num_scalar_prefetch=1,
        grid=(6, B_HEADS),
        in_specs=[],
        out_specs=pl.BlockSpec((1, 1, TQ_BLK, NB_KEYS), lambda c, h, r: (c, h, 0, 0)),
    )
    tab = pl.pallas_call(
        _nb_bias_kernel,
        out_shape=jax.ShapeDtypeStruct((6, B_HEADS, TQ_BLK, NB_KEYS), F32),
        grid_spec=grid_spec,
        compiler_params=pltpu.CompilerParams(dimension_semantics=("arbitrary", "arbitrary")),
        name="nb_bias_table",
    )(rpb.astype(F32).reshape(-1))
    return tab.reshape(6, B_HEADS // 2, 2 * TQ_BLK, NB_KEYS)


def _nb_bias_kernel(rpb_ref, o_ref):
    c = pl.program_id(0)
    h = pl.program_id(1)
    last = N_LAT_TQ - 1
    tile = jnp.where(c == 3, last - 1, jnp.where(c == 4, last, jnp.minimum(c, 2)))
    r0 = 2 * tile
    base = jnp.clip(2 * tile - 4, 0, GRID_ROWS - NB_ROWS)
    n_dr = 2 * NA_ROWS - 1
    n_dc = 2 * NA_COLS - 1
    qc = lax.broadcasted_iota(jnp.int32, (GRID_W, LANES), 0)
    lane = lax.broadcasted_iota(jnp.int32, (GRID_W, LANES), 1)
    kc = lane & (GRID_W - 1)
    second = lane >= GRID_W
    dc = kc - qc + (NA_COLS - 1)
    cs = jnp.clip(qc - NA_COLS // 2, 0, GRID_W - NA_COLS)
    col_ok = jnp.where(kc >= cs, jnp.where(kc < cs + NA_COLS, 1, 0), 0)
    for qr in range(TQ_BLK // GRID_W):
        r = r0 + qr
        rs = jnp.clip(r - NA_ROWS // 2, 0, GRID_ROWS - NA_ROWS)
        for t in range(NB_ROWS // 2):
            kr = base + 2 * t
            dr = kr - r + (NA_ROWS - 1)
            ok0 = jnp.where((kr >= rs) & (kr < rs + NA_ROWS) & (c < 5), 1, 0)
            ok1 = jnp.where((kr + 1 >= rs) & (kr + 1 < rs + NA_ROWS) & (c < 5), 1, 0)
            off0 = (h * n_dr + jnp.clip(dr, 0, n_dr - 1)) * n_dc
            off1 = (h * n_dr + jnp.clip(dr + 1, 0, n_dr - 1)) * n_dc
            acc = jnp.zeros((GRID_W, LANES), F32)
            for j in range(n_dc):
                val = jnp.where(second, rpb_ref[off1 + j], rpb_ref[off0 + j])
                acc = jnp.where(dc == j, val, acc)
            ok = col_ok * jnp.where(second, ok1, ok0)
            o_ref[0, 0, qr * GRID_W:(qr + 1) * GRID_W, t * LANES:(t + 1) * LANES] = jnp.where(
                ok > 0, acc * LOG2E, NEG_INF)


def _win_kernel(sink_ref, q_ref, kp_ref, kq_ref, kn_ref, vp_ref, vq_ref, vn_ref, kc_ref, vc_ref,
                o_ref, lhs_scr):
    tq = TQ_BLK
    nq = D_HEADS // 2
    i = pl.program_id(1)
    _expand_heads(q_ref, lhs_scr, nq, tq)
    kcat = jnp.concatenate([kp_ref[0], kq_ref[0], kn_ref[0], kc_ref[0]], axis=0)
    vcat = jnp.concatenate([vp_ref[0], vq_ref[0], vn_ref[0], vc_ref[0]], axis=0)
    span = 3 * tq
    n_keys = span + CTX_LEN
    rr = lax.broadcasted_iota(jnp.int32, (tq, n_keys), 0)
    cc = lax.broadcasted_iota(jnp.int32, (tq, n_keys), 1)
    rel = cc - tq - rr
    kpos = i * tq - tq + cc
    valid = (cc >= span) | ((jnp.abs(rel) <= D_WINDOW) & (kpos >= 0) & (kpos < SEQ))
    s = _scores(lhs_scr[...], kcat)
    outs = []
    for g in range(2 * nq):
        head = HEAD_PERM[g]
        sink = sink_ref[head]
        sg = jnp.where(valid, s[g * tq:(g + 1) * tq], NEG_INF)
        m = jnp.maximum(jnp.max(sg, axis=-1, keepdims=True), sink)
        p = jnp.exp2(sg - m)
        l = jnp.sum(p, axis=-1, keepdims=True) + jnp.exp2(sink - m)
        outs.append(jnp.dot(p.astype(BF16), vcat, preferred_element_type=F32) / l)
    o = jnp.concatenate(outs, axis=0)
    o_ref[0] = jnp.concatenate([_merge_heads(o, tq, j).astype(BF16) for j in range(nq)], axis=-1)


def _win_attn(q, k, v, sinks):
    width = D_HEADS * HEAD_DIM
    last = N_LAT_TQ - 1

    def kv_spec(off):
        return pl.BlockSpec((1, TQ_BLK, LANES), lambda b, i, s: (b, jnp.clip(i + off, 0, last), 0))

    ctx_spec = pl.BlockSpec((1, CTX_LEN, LANES), lambda b, i, s: (b, SEQ // CTX_LEN, 0))
    grid_spec = pltpu.PrefetchScalarGridSpec(
        num_scalar_prefetch=1,
        grid=(BATCH, N_LAT_TQ),
        in_specs=[pl.BlockSpec((1, TQ_BLK, width), lambda b, i, s: (b, i, 0)),
                  kv_spec(-1), kv_spec(0), kv_spec(1),
                  kv_spec(-1), kv_spec(0), kv_spec(1),
                  ctx_spec, ctx_spec],
        out_specs=pl.BlockSpec((1, TQ_BLK, width), lambda b, i, s: (b, i, 0)),
        scratch_shapes=[pltpu.VMEM((D_HEADS * TQ_BLK, LANES), BF16)],
    )
    return pl.pallas_call(
        _win_kernel,
        out_shape=jax.ShapeDtypeStruct((BATCH, SEQ, width), BF16),
        grid_spec=grid_spec,
        compiler_params=pltpu.CompilerParams(
            dimension_semantics=("arbitrary", "arbitrary"), vmem_limit_bytes=VMEM_LIMIT),
        name="window_attn",
    )(sinks, q, k, k, k, v, v, v, k, v)


def _rope_tables():
    t = jnp.arange(SEQ, dtype=jnp.int32)
    row = (t // GRID_W).astype(F32)
    col = (t % GRID_W).astype(F32)
    quarter = HEAD_DIM // 4
    inv_freq = ROPE_THETA ** (-jnp.arange(quarter, dtype=F32) / quarter)
    ang_r = row[:, None] * inv_freq
    ang_c = col[:, None] * inv_freq
    cr, sr, cc, sc = jnp.cos(ang_r), jnp.sin(ang_r), jnp.cos(ang_c), jnp.sin(ang_c)
    cos = jnp.tile(jnp.concatenate([cr, cr, cc, cc], axis=-1), (1, LANES // HEAD_DIM))
    sin = jnp.tile(jnp.concatenate([-sr, sr, -sc, sc], axis=-1), (1, LANES // HEAD_DIM))
    cos = jnp.concatenate([cos, jnp.ones((CTX_LEN, LANES), F32)], axis=0)
    sin = jnp.concatenate([sin, jnp.zeros((CTX_LEN, LANES), F32)], axis=0)
    return cos, sin


def _head_cols(perm):
    return np.concatenate([np.arange(HEAD_DIM) + HEAD_DIM * h for h in perm])


def kernel(x, c, ctx, c_ctx, w_mod, b_mod, w_in_even, w_out_even, a_q_norm, a_k_norm, b_rpb,
           w_in_odd, w_out_odd, c_lambda, c_subln, d_sinks, final_norm):
    assert DEPTH == 2 and x.shape == (BATCH, SEQ, D_MODEL) and ctx.shape == (BATCH, CTX_LEN, D_MODEL)
    perm = _head_cols(HEAD_PERM)
    hw = A_HEADS * HEAD_DIM

    cs = jnp.concatenate([c, c_ctx[None], jnp.zeros((8 - BATCH - 1, D_MODEL), F32)], axis=0)
    mod = _modulation(cs, w_mod, b_mod)
    mod0 = mod[0].reshape(8, 1, 3 * D_MODEL)
    mod1 = mod[1].reshape(8, 1, 3 * D_MODEL)

    cos, sin = _rope_tables()
    seg = jnp.asarray(np.kron(np.eye(LANES // HEAD_DIM), np.ones((HEAD_DIM, HEAD_DIM))), BF16)
    tile2 = LANES // HEAD_DIM
    qg = jnp.tile(a_q_norm[0].astype(F32), tile2).reshape(1, LANES)
    kg = jnp.tile(a_k_norm[0].astype(F32), tile2).reshape(1, LANES)

    z0 = 2304
    cols0 = np.concatenate([perm, np.arange(hw, z0), z0 + perm, np.arange(z0 + hw, z0 + 2 * hw)])
    w_in0 = w_in_even[0][:, cols0].astype(BF16)
    w_out0 = w_out_even[0][np.concatenate([perm, np.arange(hw, 2 * hw)])].astype(BF16)
    plan0 = (
        (hw, ((0, 0, hw, "q", True, True, False),)),
        (2 * LANES, ((1, 0, LANES, "k", True, False, False),
                     (2, LANES, 2 * LANES, None, False, False, True))),
        (hw, ((3, 0, hw, None, False, True, False),)),
        (hw, ((4, 0, hw, None, False, False, False),)),
        (hw, ((5, 0, hw, None, False, False, False),)),
        (2 * hw, ((6, 0, 2 * hw, None, False, False, False),)),
    )
    qa, ka, va, qb, kb, vb, zz = _in_proj(x, ctx, mod0, w_in0, qg, kg, cos, sin, seg, plan0,
                                          (hw, LANES, LANES, hw, hw, hw, 2 * hw))
    ya = _dense_attn(qa, ka, va, nq=4, tq=128, tk=DENSE_TK, n_q_tok=TOK)
    yb = _nb_attn(qb, kb, vb, _nb_bias_table(b_rpb[0]))
    x1 = _out_proj(x, ctx, ya, yb, zz, w_out0, mod0, None)

    cols1 = np.concatenate([np.arange(0, 3 * hw), 3 * hw + perm, np.arange(4 * hw, z0 + hw), z0 + hw + perm])
    w_in1 = w_in_odd[0][:, cols1].astype(BF16)
    w_out1 = w_out_odd[0][np.concatenate([np.arange(hw), hw + perm])].astype(BF16)
    plan1 = (
        (hw, ((0, 0, hw, None, True, True, False),)),
        (hw, ((1, 0, hw, None, True, False, False),)),
        (hw, ((2, 0, hw, None, False, False, True),)),
        (hw, ((3, 0, hw, None, True, True, False),)),
        (2 * LANES, ((4, 0, LANES, None, True, False, False),
                     (5, LANES, 2 * LANES, None, False, False, False))),
        (2 * hw, ((6, 0, 2 * hw, None, False, False, False),)),
    )
    qc, kc, vc, qd, kd, vd, zz1 = _in_proj(x1, None, mod1, w_in1, qg, kg, cos, sin, seg, plan1,
                                           (hw, hw, hw, hw, LANES, LANES, 2 * hw))
    yc = _dense_attn(qc, kc, vc, nq=1, tq=512, tk=DENSE_TK, n_q_tok=SEQ, diff=True,
                     lamv=c_lambda[0].astype(F32), subln=c_subln[0].astype(F32).reshape(1, LANES),
                     lam0=_lambda_init(1))
    yd = _win_attn(qd, kd, vd, d_sinks[0].astype(F32) * LOG2E)
    return _out_proj(x1, None, yc, yd, zz1, w_out1, mod1, final_norm.astype(F32).reshape(1, D_MODEL))
```

```python
import functools
import math

import numpy as np
import jax
import jax.numpy as jnp
from jax import lax
from jax.experimental import pallas as pl
from jax.experimental.pallas import tpu as pltpu

D_MODEL = 1024
BATCH = 4
SEQ = 8192
DEPTH = 2
GRID_W = 64
CTX_LEN = 256
HEAD_DIM = 64
ROPE_THETA = 10000.0
EPS = 1e-6
NEG_INF = -1e30
A_HEADS = 8
B_HEADS = 8
NA_ROWS = 8
NA_COLS = 16
C_HEADS = 4
D_HEADS = 8
D_WINDOW = 128
SCALE = HEAD_DIM ** -0.5
LOG2E = math.log2(math.e)
QSCALE = SCALE * LOG2E

LANES = 128
HALF = HEAD_DIM
TOK = SEQ + CTX_LEN
TM = 256
N_LAT_TM = SEQ // TM
VT = 128
QB = 256
DENSE_TK = 1024
TQ_BLK = 128
N_LAT_TQ = SEQ // TQ_BLK
N_CTX_TQ = CTX_LEN // TQ_BLK
GRID_ROWS = SEQ // GRID_W
NB_ROWS = 10
NB_KEYS = NB_ROWS * GRID_W
NB_BLKS = NB_KEYS // TQ_BLK
WIN_BLKS = 1 + 2 * (D_WINDOW // TQ_BLK)
VMEM_LIMIT = 56 * 1024 * 1024

HEAD_PERM = (0, 4, 1, 5, 2, 6, 3, 7)
F32 = jnp.float32
BF16 = jnp.bfloat16


def _lambda_init(layer):
    return 0.8 - 0.6 * math.exp(-0.3 * layer)


def _lane_iota(rows):
    return lax.broadcasted_iota(jnp.int32, (rows, LANES), 1)


def _silu(z):
    return z / (1.0 + jnp.exp(-z))


def _elements(*sizes):
    return tuple(pl.Element(s) for s in sizes)


def _mod_kernel(c_ref, w_ref, b_ref, o_ref):
    o_ref[0] = jnp.dot(_silu(c_ref[...]), w_ref[0], preferred_element_type=F32) + b_ref[0]


def _modulation(cs, w_mod, b_mod):
    tn = D_MODEL
    return pl.pallas_call(
        _mod_kernel,
        out_shape=jax.ShapeDtypeStruct((DEPTH, 8, 3 * D_MODEL), F32),
        grid=(DEPTH, 3 * D_MODEL // tn),
        in_specs=[
            pl.BlockSpec((8, D_MODEL), lambda l, j: (0, 0)),
            pl.BlockSpec((1, D_MODEL, tn), lambda l, j: (l, 0, j)),
            pl.BlockSpec((1, 1, tn), lambda l, j: (l, 0, j)),
        ],
        out_specs=pl.BlockSpec((1, 8, tn), lambda l, j: (l, 0, j)),
        compiler_params=pltpu.CompilerParams(
            dimension_semantics=("arbitrary", "arbitrary"), vmem_limit_bytes=VMEM_LIMIT),
        name="modulation",
    )(cs, w_mod, b_mod.reshape(DEPTH, 1, 3 * D_MODEL))


def _swap16(x):
    lane = _lane_iota(x.shape[0])
    fwd = pltpu.roll(x, LANES - 16, 1)
    bwd = pltpu.roll(x, 16, 1)
    return jnp.where((lane & 16) == 0, fwd, bwd)


def _proj_kernel(*refs, plan, dual):
    if dual:
        x_ref, xc_ref = refs[:2]
        refs = refs[2:]
    else:
        x_ref = refs[0]
        refs = refs[1:]
    mod_ref, w_ref, qg_ref, kg_ref, cos_ref, sin_ref, seg_ref = refs[:7]
    out_refs = refs[7:]

    xt = x_ref[0]
    if dual:
        xt = jnp.where(pl.program_id(1) < N_LAT_TM, xt, xc_ref[0])
    ms = jnp.mean(xt * xt, axis=-1, keepdims=True)
    xn = xt * lax.rsqrt(ms + EPS)
    shift = mod_ref[0, :, 0:D_MODEL]
    scale = mod_ref[0, :, D_MODEL:2 * D_MODEL]
    h = (xn * (1.0 + scale) + shift).astype(BF16)

    cos = cos_ref[...]
    sin = sin_ref[...]
    seg = seg_ref[...]
    col = 0
    for width, outs in plan:
        acc = jnp.dot(h, w_ref[:, col:col + width], preferred_element_type=F32)
        col += width
        for out_idx, lo, hi, norm, rope, scale_q, transpose in outs:
            pieces = []
            for c0 in range(lo, hi, LANES):
                blk = acc[:, c0:c0 + LANES]
                if transpose:
                    blk_t = blk.T.astype(BF16)
                    for u in range(TM // VT):
                        out_refs[out_idx][0, (c0 - lo) // LANES, u] = blk_t[:, u * VT:(u + 1) * VT]
                    continue
                if norm:
                    sq = blk * blk
                    sq_hi = sq.astype(BF16)
                    sq_lo = (sq - sq_hi.astype(F32)).astype(BF16)
                    ssum = (jnp.dot(sq_hi, seg, preferred_element_type=F32)
                            + jnp.dot(sq_lo, seg, preferred_element_type=F32))
                    gain = qg_ref[...] if norm == "q" else kg_ref[...]
                    blk = blk * lax.rsqrt(ssum * (1.0 / HEAD_DIM) + EPS) * gain
                if rope:
                    blk = blk * cos + _swap16(blk) * sin
                if scale_q:
                    blk = blk * QSCALE
                pieces.append(blk.astype(BF16))
            if pieces:
                out_refs[out_idx][0] = pieces[0] if len(pieces) == 1 else jnp.concatenate(pieces, axis=-1)


def _in_proj(x, xc, mod, w, qg, kg, cos, sin, seg, plan, out_widths):
    dual = xc is not None
    n_tiles = TOK // TM
    if dual:
        x_specs = [
            pl.BlockSpec((1, TM, D_MODEL), lambda b, i: (b, jnp.minimum(i, N_LAT_TM - 1), 0)),
            pl.BlockSpec((1, CTX_LEN, D_MODEL), lambda b, i: (b, 0, 0)),
        ]
        xs = (x, xc)
    else:
        x_specs = [pl.BlockSpec((1, TM, D_MODEL), lambda b, i: (b, i, 0))]
        xs = (x,)
    n_in = w.shape[1]
    in_specs = x_specs + [
        pl.BlockSpec((1, 1, 3 * D_MODEL), lambda b, i: (jnp.where(i < N_LAT_TM, b, BATCH), 0, 0)),
        pl.BlockSpec((D_MODEL, n_in), lambda b, i: (0, 0)),
        pl.BlockSpec((1, LANES), lambda b, i: (0, 0)),
        pl.BlockSpec((1, LANES), lambda b, i: (0, 0)),
        pl.BlockSpec((TM, LANES), lambda b, i: (i, 0)),
        pl.BlockSpec((TM, LANES), lambda b, i: (i, 0)),
        pl.BlockSpec((LANES, LANES), lambda b, i: (0, 0)),
    ]
    transposed = {o[0] for _, outs in plan for o in outs if o[6]}
    out_shape, out_specs = [], []
    for idx, wd in enumerate(out_widths):
        if idx in transposed:
            out_shape.append(jax.ShapeDtypeStruct((BATCH, wd // LANES, TOK // VT, LANES, VT), BF16))
            out_specs.append(pl.BlockSpec((1, wd // LANES, TM // VT, LANES, VT),
                                          lambda b, i: (b, 0, i, 0, 0)))
        else:
            out_shape.append(jax.ShapeDtypeStruct((BATCH, TOK, wd), BF16))
            out_specs.append(pl.BlockSpec((1, TM, wd), lambda b, i: (b, i, 0)))
    return pl.pallas_call(
        functools.partial(_proj_kernel, plan=plan, dual=dual),
        out_shape=tuple(out_shape),
        grid=(BATCH, n_tiles),
        in_specs=in_specs,
        out_specs=tuple(out_specs),
        compiler_params=pltpu.CompilerParams(
            dimension_semantics=("arbitrary", "arbitrary"), vmem_limit_bytes=VMEM_LIMIT),
        name="in_proj",
    )(*xs, mod, w, qg, kg, cos, sin, seg)


def _out_kernel(*refs, dual, final):
    if dual:
        x_ref, xc_ref = refs[:2]
        refs = refs[2:]
    else:
        x_ref = refs[0]
        refs = refs[1:]
    y1_ref, y2_ref, z_ref, w_ref, mod_ref = refs[:5]
    refs = refs[5:]
    if final:
        fn_ref, o_ref = refs
    else:
        (o_ref,) = refs

    xt = x_ref[0]
    if dual:
        xt = jnp.where(pl.program_id(1) < N_LAT_TM, xt, xc_ref[0])
    y = jnp.concatenate([y1_ref[0], y2_ref[0]], axis=-1).astype(F32)
    g = (y * _silu(z_ref[0].astype(F32))).astype(BF16)
    d = jnp.dot(g, w_ref[...], preferred_element_type=F32)
    gate = mod_ref[0, :, 2 * D_MODEL:3 * D_MODEL]
    xo = xt + gate * d
    if final:
        ms = jnp.mean(xo * xo, axis=-1, keepdims=True)
        xo = xo * lax.rsqrt(ms + EPS) * fn_ref[...]
    o_ref[0] = xo


def _out_proj(x, xc, y1, y2, z, w, mod, final_gain):
    dual = xc is not None
    final = final_gain is not None
    n_tok = SEQ if final else TOK
    half = y1.shape[-1]
    if dual:
        x_specs = [
            pl.BlockSpec((1, TM, D_MODEL), lambda b, i: (b, jnp.minimum(i, N_LAT_TM - 1), 0)),
            pl.BlockSpec((1, CTX_LEN, D_MODEL), lambda b, i: (b, 0, 0)),
        ]
        xs = (x, xc)
    else:
        x_specs = [pl.BlockSpec((1, TM, D_MODEL), lambda b, i: (b, i, 0))]
        xs = (x,)
    in_specs = x_specs + [
        pl.BlockSpec((1, TM, half), lambda b, i: (b, i, 0)),
        pl.BlockSpec((1, TM, half), lambda b, i: (b, i, 0)),
        pl.BlockSpec((1, TM, 2 * half), lambda b, i: (b, i, 0)),
        pl.BlockSpec((2 * half, D_MODEL), lambda b, i: (0, 0)),
        pl.BlockSpec((1, 1, 3 * D_MODEL), lambda b, i: (jnp.where(i < N_LAT_TM, b, BATCH), 0, 0)),
    ]
    args = xs + (y1, y2, z, w, mod)
    if final:
        in_specs.append(pl.BlockSpec((1, D_MODEL), lambda b, i: (0, 0)))
        args = args + (final_gain,)
    return pl.pallas_call(
        functools.partial(_out_kernel, dual=dual, final=final),
        out_shape=jax.ShapeDtypeStruct((BATCH, n_tok, D_MODEL), F32),
        grid=(BATCH, n_tok // TM),
        in_specs=in_specs,
        out_specs=pl.BlockSpec((1, TM, D_MODEL), lambda b, i: (b, i, 0)),
        compiler_params=pltpu.CompilerParams(
            dimension_semantics=("arbitrary", "arbitrary"), vmem_limit_bytes=VMEM_LIMIT),
        name="out_proj",
    )(*args)


def _write_qt(q_ref, qt_scr, nq, tq):
    feat = lax.broadcasted_iota(jnp.int32, (LANES, tq), 0)
    for j in range(nq):
        qt = q_ref[0, :, j * LANES:(j + 1) * LANES].astype(F32).T
        qt_scr[:, (2 * j) * tq:(2 * j + 1) * tq] = jnp.where(feat < HALF, qt, 0.0).astype(BF16)
        qt_scr[:, (2 * j + 1) * tq:(2 * j + 2) * tq] = jnp.where(feat >= HALF, qt, 0.0).astype(BF16)


def _merge_qt(ot, nq, tq):
    feat = lax.broadcasted_iota(jnp.int32, (LANES, tq), 0)
    pieces = []
    for j in range(nq):
        blk = jnp.where(feat < HALF, ot[:, (2 * j) * tq:(2 * j + 1) * tq],
                        ot[:, (2 * j + 1) * tq:(2 * j + 2) * tq])
        pieces.append(blk.T.astype(BF16))
    return pieces[0] if nq == 1 else jnp.concatenate(pieces, axis=-1)


def _dense_kernel(*refs, nq, tq, tk, n_lat_tiles, diff, lam0):
    if diff:
        q_ref, k_ref, vt_ref, lamv_ref, subln_ref, o_ref, qt_scr, m_scr, l_scr, acc_scr, s_scr = refs
    else:
        q_ref, k_ref, vt_ref, o_ref, qt_scr, m_scr, l_scr, acc_scr, s_scr = refs
    n_q = 2 * nq * tq
    _write_qt(q_ref, qt_scr, nq, tq)
    m_scr[...] = jnp.full((1, n_q), -jnp.inf, F32)
    l_scr[...] = jnp.zeros((1, n_q), F32)
    acc_scr[...] = jnp.zeros((LANES, n_q), F32)

    n_blk = n_q // QB
    ctx_chunk = (SEQ, SEQ // VT, CTX_LEN // VT)
    lat_chunks = [(c * tk, c * (tk // VT), tk // VT) for c in range(SEQ // tk)]

    def scores(chunk, n, half):
        start, _, subs = chunk
        kc = k_ref[0, pl.ds(start, subs * VT), :]
        s_scr[half, n, 0:subs * VT, :] = jnp.dot(kc, qt_scr[:, n * QB:(n + 1) * QB],
                                                 preferred_element_type=F32)

    def softmax_pv(chunk, n, half):
        _, tile0, subs = chunk
        cols = slice(n * QB, (n + 1) * QB)
        vtc = jnp.concatenate([vt_ref[0, 0, tile0 + u] for u in range(subs)], axis=1)
        s = s_scr[half, n, 0:subs * VT, :]
        m_prev = m_scr[:, cols]
        m_new = jnp.maximum(m_prev, jnp.max(s, axis=0, keepdims=True))
        alpha = jnp.exp2(m_prev - m_new)
        p = jnp.exp2(s - m_new)
        l_scr[:, cols] = alpha * l_scr[:, cols] + jnp.sum(p, axis=0, keepdims=True)
        acc_scr[:, cols] = alpha * acc_scr[:, cols] + jnp.dot(
            vtc, p.astype(BF16), preferred_element_type=F32)
        m_scr[:, cols] = m_new

    def half_step(cur, half, nxt):
        for n in range(n_blk):
            if nxt is not None:
                scores(nxt, n, 1 - half)
            softmax_pv(cur, n, half)

    def latent_keys():
        half_step(ctx_chunk, 0, lat_chunks[0])
        for c, chunk in enumerate(lat_chunks):
            half_step(chunk, (c + 1) % 2, lat_chunks[c + 1] if c + 1 < len(lat_chunks) else None)

    def context_keys_only():
        half_step(ctx_chunk, 0, None)

    for n in range(n_blk):
        scores(ctx_chunk, n, 0)
    is_latent = pl.program_id(2) < n_lat_tiles
    pl.when(is_latent)(latent_keys)
    pl.when(jnp.logical_not(is_latent))(context_keys_only)

    ot = acc_scr[...] * (1.0 / l_scr[...])
    if diff:
        lv = lamv_ref[...]
        lam = (jnp.exp(jnp.sum(lv[0:1] * lv[1:2], axis=-1, keepdims=True))
               - jnp.exp(jnp.sum(lv[2:3] * lv[3:4], axis=-1, keepdims=True)) + lam0)
        dlt = (ot[:, 0:tq] - lam * ot[:, tq:2 * tq]).T
        ms = jnp.mean(dlt * dlt, axis=-1, keepdims=True)
        y = dlt * lax.rsqrt(ms + EPS) * subln_ref[...] * (1.0 - lam0)
        o_ref[0] = y.astype(BF16)
    else:
        o_ref[0] = _merge_qt(ot, nq, tq)


def _dense_attn(q, k, vt, *, nq, tq, tk, n_q_tok, diff=False, lamv=None, subln=None, lam0=0.0):
    groups = q.shape[-1] // (nq * LANES)
    n_tiles = n_q_tok // tq
    n_lat_tiles = SEQ // tq
    n_q = 2 * nq * tq
    in_specs = [
        pl.BlockSpec((1, tq, nq * LANES), lambda b, g, i: (b, i, g)),
        pl.BlockSpec((1, TOK, LANES), lambda b, g, i: (b, 0, g)),
        pl.BlockSpec((1, 1, TOK // VT, LANES, VT), lambda b, g, i: (b, g, 0, 0, 0)),
    ]
    args = (q, k, vt)
    if diff:
        in_specs += [pl.BlockSpec((4, HEAD_DIM), lambda b, g, i: (0, 0)),
                     pl.BlockSpec((1, LANES), lambda b, g, i: (0, 0))]
        args = args + (lamv, subln)
    return pl.pallas_call(
        functools.partial(_dense_kernel, nq=nq, tq=tq, tk=tk, n_lat_tiles=n_lat_tiles,
                          diff=diff, lam0=lam0),
        out_shape=jax.ShapeDtypeStruct((BATCH, n_q_tok, groups * nq * LANES), BF16),
        grid=(BATCH, groups, n_tiles),
        in_specs=in_specs,
        out_specs=pl.BlockSpec((1, tq, nq * LANES), lambda b, g, i: (b, i, g)),
        scratch_shapes=[pltpu.VMEM((LANES, n_q), BF16),
                        pltpu.VMEM((1, n_q), F32),
                        pltpu.VMEM((1, n_q), F32),
                        pltpu.VMEM((LANES, n_q), F32),
                        pltpu.VMEM((2, n_q // QB, tk, QB), F32)],
        compiler_params=pltpu.CompilerParams(
            dimension_semantics=("arbitrary", "arbitrary", "arbitrary"),
            vmem_limit_bytes=VMEM_LIMIT),
        name="dense_diff_attn" if diff else "dense_gqa_attn",
    )(*args)


def _nb_kernel(q_ref, k_ref, vt_ref, kc_ref, vc_ref, bias_ref, o_ref, qt_scr, s_scr):
    tq = TQ_BLK
    nq = B_HEADS // 2
    n_keys = NB_KEYS + CTX_LEN
    _write_qt(q_ref, qt_scr, nq, tq)
    for j in range(nq):
        cs = slice(j * LANES, (j + 1) * LANES)
        kcat = jnp.concatenate([k_ref[0, :, cs], kc_ref[0, :, cs]], axis=0)
        s_scr[j] = jnp.dot(kcat, qt_scr[:, (2 * j) * tq:(2 * j + 2) * tq], preferred_element_type=F32)
    outs = []
    for j in range(nq):
        vtcat = jnp.concatenate([vt_ref[0, j, u] for u in range(NB_BLKS)]
                                + [vc_ref[0, j, u] for u in range(CTX_LEN // VT)], axis=1)
        s_n = s_scr[j, 0:NB_KEYS, :] + bias_ref[0, j]
        s_c = s_scr[j, NB_KEYS:n_keys, :]
        m = jnp.maximum(jnp.max(s_n, axis=0, keepdims=True), jnp.max(s_c, axis=0, keepdims=True))
        p = jnp.concatenate([jnp.exp2(s_n - m), jnp.exp2(s_c - m)], axis=0)
        l = jnp.sum(p, axis=0, keepdims=True)
        outs.append(jnp.dot(vtcat, p.astype(BF16), preferred_element_type=F32) * (1.0 / l))
    o_ref[0] = _merge_qt(jnp.concatenate(outs, axis=1), nq, tq)


def _nb_case(i):
    last = N_LAT_TQ - 1
    return jnp.where(i == 0, 0, jnp.where(i == 1, 1, jnp.where(
        i == last - 1, 3, jnp.where(i == last, 4, jnp.where(i > last, 5, 2)))))


def _nb_attn(q, k, vt, bias):
    width = B_HEADS * HEAD_DIM
    n_pairs = B_HEADS // 2
    n_tiles = N_LAT_TQ + N_CTX_TQ

    def first_blk(i):
        return jnp.clip(i - 2, 0, N_LAT_TQ - NB_BLKS)

    in_specs = [pl.BlockSpec((1, TQ_BLK, width), lambda b, i: (b, i, 0)),
                pl.BlockSpec(_elements(1, NB_KEYS, width), lambda b, i: (b, first_blk(i) * TQ_BLK, 0)),
                pl.BlockSpec(_elements(1, n_pairs, NB_BLKS, LANES, VT),
                             lambda b, i: (b, 0, first_blk(i), 0, 0)),
                pl.BlockSpec((1, CTX_LEN, width), lambda b, i: (b, SEQ // CTX_LEN, 0)),
                pl.BlockSpec((1, n_pairs, CTX_LEN // VT, LANES, VT),
                             lambda b, i: (b, 0, SEQ // CTX_LEN, 0, 0)),
                pl.BlockSpec((1, n_pairs, NB_KEYS, 2 * TQ_BLK), lambda b, i: (_nb_case(i), 0, 0, 0))]
    return pl.pallas_call(
        _nb_kernel,
        out_shape=jax.ShapeDtypeStruct((BATCH, TOK, width), BF16),
        grid=(BATCH, n_tiles),
        in_specs=in_specs,
        out_specs=pl.BlockSpec((1, TQ_BLK, width), lambda b, i: (b, i, 0)),
        scratch_shapes=[pltpu.VMEM((LANES, B_HEADS * TQ_BLK), BF16),
                        pltpu.VMEM((n_pairs, NB_KEYS + CTX_LEN, 2 * TQ_BLK), F32)],
        compiler_params=pltpu.CompilerParams(
            dimension_semantics=("arbitrary", "arbitrary"), vmem_limit_bytes=VMEM_LIMIT),
        name="neighbourhood_attn",
    )(q, k, vt, k, vt, bias)


def _nb_bias_table(rpb):
    grid_spec = pltpu.PrefetchScalarGridSpec(
        num_scalar_prefetch=1,
        grid=(6, B_HEADS),
        in_specs=[],
        out_specs=pl.BlockSpec((1, 1, NB_KEYS, TQ_BLK), lambda c, h, r: (c, h // 2, 0, h % 2)),
    )
    return pl.pallas_call(
        _nb_bias_kernel,
        out_shape=jax.ShapeDtypeStruct((6, B_HEADS // 2, NB_KEYS, 2 * TQ_BLK), F32),
        grid_spec=grid_spec,
        compiler_params=pltpu.CompilerParams(dimension_semantics=("arbitrary", "arbitrary")),
        name="nb_bias_table",
    )(rpb.astype(F32).reshape(-1))


def _nb_bias_kernel(rpb_ref, o_ref):
    c = pl.program_id(0)
    h = pl.program_id(1)
    last = N_LAT_TQ - 1
    tile = jnp.where(c == 3, last - 1, jnp.where(c == 4, last, jnp.minimum(c, 2)))
    r0 = 2 * tile
    base = jnp.clip(2 * tile - 4, 0, GRID_ROWS - NB_ROWS)
    n_dr = 2 * NA_ROWS - 1
    n_dc = 2 * NA_COLS - 1
    kc = lax.broadcasted_iota(jnp.int32, (GRID_W, LANES), 0)
    lane = lax.broadcasted_iota(jnp.int32, (GRID_W, LANES), 1)
    qc = lane & (GRID_W - 1)
    second = lane >= GRID_W
    dc = kc - qc + (NA_COLS - 1)
    cs = jnp.clip(qc - NA_COLS // 2, 0, GRID_W - NA_COLS)
    col_ok = jnp.where(kc >= cs, jnp.where(kc < cs + NA_COLS, 1, 0), 0)
    rs0 = jnp.clip(r0 - NA_ROWS // 2, 0, GRID_ROWS - NA_ROWS)
    rs1 = jnp.clip(r0 + 1 - NA_ROWS // 2, 0, GRID_ROWS - NA_ROWS)
    for krow in range(NB_ROWS):
        kr = base + krow
        dr = kr - r0 + (NA_ROWS - 1)
        ok0 = jnp.where((kr >= rs0) & (kr < rs0 + NA_ROWS) & (c < 5), 1, 0)
        ok1 = jnp.where((kr >= rs1) & (kr < rs1 + NA_ROWS) & (c < 5), 1, 0)
        off0 = (h * n_dr + jnp.clip(dr, 0, n_dr - 1)) * n_dc
        off1 = (h * n_dr + jnp.clip(dr - 1, 0, n_dr - 1)) * n_dc
        acc = jnp.zeros((GRID_W, LANES), F32)
        for j in range(n_dc):
            val = jnp.where(second, rpb_ref[off1 + j], rpb_ref[off0 + j])
            acc = jnp.where(dc == j, val, acc)
        ok = col_ok * jnp.where(second, ok1, ok0)
        o_ref[0, 0, krow * GRID_W:(krow + 1) * GRID_W, :] = jnp.where(ok > 0, acc * LOG2E, NEG_INF)


def _win_first_blk(i):
    return jnp.clip(i - D_WINDOW // TQ_BLK, 0, N_LAT_TQ - WIN_BLKS)


def _win_kernel(sink_ref, q_ref, k_ref, kc_ref, vt_ref, vc_ref, o_ref, qt_scr, s_scr):
    tq = TQ_BLK
    nq = D_HEADS // 2
    n_blk = 2 * nq * tq // QB
    i = pl.program_id(1)
    _write_qt(q_ref, qt_scr, nq, tq)
    kcat = jnp.concatenate([k_ref[0], kc_ref[0]], axis=0)
    vtcat = jnp.concatenate([vt_ref[0, 0, u] for u in range(WIN_BLKS)]
                            + [vc_ref[0, 0, u] for u in range(CTX_LEN // VT)], axis=1)
    for n in range(n_blk):
        s_scr[n] = jnp.dot(kcat, qt_scr[:, n * QB:(n + 1) * QB], preferred_element_type=F32)

    span = WIN_BLKS * tq
    n_keys = span + CTX_LEN
    kk = lax.broadcasted_iota(jnp.int32, (n_keys, QB), 0)
    qq = lax.broadcasted_iota(jnp.int32, (n_keys, QB), 1) & (tq - 1)
    rel = kk - qq + (_win_first_blk(i) - i) * tq
    mask = jnp.where(kk >= span, 0.0, jnp.where(jnp.abs(rel) <= D_WINDOW, 0.0, NEG_INF))

    outs = []
    for n in range(n_blk):
        sinks = jnp.concatenate(
            [jnp.full((1, tq), sink_ref[HEAD_PERM[n * QB // tq + g]], F32) for g in range(QB // tq)], axis=1)
        s = s_scr[n] + mask
        m = jnp.maximum(jnp.max(s, axis=0, keepdims=True), sinks)
        p = jnp.exp2(s - m)
        l = jnp.sum(p, axis=0, keepdims=True) + jnp.exp2(sinks - m)
        outs.append(jnp.dot(vtcat, p.astype(BF16), preferred_element_type=F32) * (1.0 / l))
    o_ref[0] = _merge_qt(jnp.concatenate(outs, axis=1), nq, tq)


def _win_attn(q, k, vt, sinks):
    width = D_HEADS * HEAD_DIM
    n_q = D_HEADS * TQ_BLK
    grid_spec = pltpu.PrefetchScalarGridSpec(
        num_scalar_prefetch=1,
        grid=(BATCH, N_LAT_TQ),
        in_specs=[pl.BlockSpec((1, TQ_BLK, width), lambda b, i, s: (b, i, 0)),
                  pl.BlockSpec(_elements(1, WIN_BLKS * TQ_BLK, LANES),
                               lambda b, i, s: (b, _win_first_blk(i) * TQ_BLK, 0)),
                  pl.BlockSpec((1, CTX_LEN, LANES), lambda b, i, s: (b, SEQ // CTX_LEN, 0)),
                  pl.BlockSpec(_elements(1, 1, WIN_BLKS, LANES, VT),
                               lambda b, i, s: (b, 0, _win_first_blk(i), 0, 0)),
                  pl.BlockSpec((1, 1, CTX_LEN // VT, LANES, VT), lambda b, i, s: (b, 0, SEQ // CTX_LEN, 0, 0))],
        out_specs=pl.BlockSpec((1, TQ_BLK, width), lambda b, i, s: (b, i, 0)),
        scratch_shapes=[pltpu.VMEM((LANES, n_q), BF16),
                        pltpu.VMEM((n_q // QB, WIN_BLKS * TQ_BLK + CTX_LEN, QB), F32)],
    )
    return pl.pallas_call(
        _win_kernel,
        out_shape=jax.ShapeDtypeStruct((BATCH, SEQ, width), BF16),
        grid_spec=grid_spec,
        compiler_params=pltpu.CompilerParams(
            dimension_semantics=("arbitrary", "arbitrary"), vmem_limit_bytes=VMEM_LIMIT),
        name="window_attn",
    )(sinks, q, k, k, vt, vt)


def _rope_tables():
    t = jnp.arange(SEQ, dtype=jnp.int32)
    row = (t // GRID_W).astype(F32)
    col = (t % GRID_W).astype(F32)
    quarter = HEAD_DIM // 4
    inv_freq = ROPE_THETA ** (-jnp.arange(quarter, dtype=F32) / quarter)
    ang_r = row[:, None] * inv_freq
    ang_c = col[:, None] * inv_freq
    cr, sr, cc, sc = jnp.cos(ang_r), jnp.sin(ang_r), jnp.cos(ang_c), jnp.sin(ang_c)
    cos = jnp.tile(jnp.concatenate([cr, cr, cc, cc], axis=-1), (1, LANES // HEAD_DIM))
    sin = jnp.tile(jnp.concatenate([-sr, sr, -sc, sc], axis=-1), (1, LANES // HEAD_DIM))
    cos = jnp.concatenate([cos, jnp.ones((CTX_LEN, LANES), F32)], axis=0)
    sin = jnp.concatenate([sin, jnp.zeros((CTX_LEN, LANES), F32)], axis=0)
    return cos, sin


def _head_cols(perm):
    return np.concatenate([np.arange(HEAD_DIM) + HEAD_DIM * h for h in perm])


def kernel(x, c, ctx, c_ctx, w_mod, b_mod, w_in_even, w_out_even, a_q_norm, a_k_norm, b_rpb,
           w_in_odd, w_out_odd, c_lambda, c_subln, d_sinks, final_norm):
    assert DEPTH == 2 and x.shape == (BATCH, SEQ, D_MODEL) and ctx.shape == (BATCH, CTX_LEN, D_MODEL)
    perm = _head_cols(HEAD_PERM)
    hw = A_HEADS * HEAD_DIM

    cs = jnp.concatenate([c, c_ctx[None], jnp.zeros((8 - BATCH - 1, D_MODEL), F32)], axis=0)
    mod = _modulation(cs, w_mod, b_mod)
    mod0 = mod[0].reshape(8, 1, 3 * D_MODEL)
    mod1 = mod[1].reshape(8, 1, 3 * D_MODEL)

    cos, sin = _rope_tables()
    seg = jnp.asarray(np.kron(np.eye(LANES // HEAD_DIM), np.ones((HEAD_DIM, HEAD_DIM))), BF16)
    tile2 = LANES // HEAD_DIM
    qg = jnp.tile(a_q_norm[0].astype(F32), tile2).reshape(1, LANES)
    kg = jnp.tile(a_k_norm[0].astype(F32), tile2).reshape(1, LANES)

    z0 = 2304
    cols0 = np.concatenate([perm, np.arange(hw, z0), z0 + perm, np.arange(z0 + hw, z0 + 2 * hw)])
    w_in0 = w_in_even[0][:, cols0].astype(BF16)
    w_out0 = w_out_even[0][np.concatenate([perm, np.arange(hw, 2 * hw)])].astype(BF16)
    plan0 = (
        (hw, ((0, 0, hw, "q", True, True, False),)),
        (2 * LANES, ((1, 0, LANES, "k", True, False, False),
                     (2, LANES, 2 * LANES, None, False, False, True))),
        (hw, ((3, 0, hw, None, False, True, False),)),
        (hw, ((4, 0, hw, None, False, False, False),)),
        (hw, ((5, 0, hw, None, False, False, True),)),
        (2 * hw, ((6, 0, 2 * hw, None, False, False, False),)),
    )
    qa, ka, va, qb, kb, vb, zz = _in_proj(x, ctx, mod0, w_in0, qg, kg, cos, sin, seg, plan0,
                                          (hw, LANES, LANES, hw, hw, hw, 2 * hw))
    ya = _dense_attn(qa, ka, va, nq=4, tq=128, tk=DENSE_TK, n_q_tok=TOK)
    yb = _nb_attn(qb, kb, vb, _nb_bias_table(b_rpb[0]))
    x1 = _out_proj(x, ctx, ya, yb, zz, w_out0, mod0, None)

    cols1 = np.concatenate([np.arange(0, 3 * hw), 3 * hw + perm, np.arange(4 * hw, z0 + hw), z0 + hw + perm])
    w_in1 = w_in_odd[0][:, cols1].astype(BF16)
    w_out1 = w_out_odd[0][np.concatenate([np.arange(hw), hw + perm])].astype(BF16)
    plan1 = (
        (hw, ((0, 0, hw, None, True, True, False),)),
        (hw, ((1, 0, hw, None, True, False, False),)),
        (hw, ((2, 0, hw, None, False, False, True),)),
        (hw, ((3, 0, hw, None, True, True, False),)),
        (2 * LANES, ((4, 0, LANES, None, True, False, False),
                     (5, LANES, 2 * LANES, None, False, False, True))),
        (2 * hw, ((6, 0, 2 * hw, None, False, False, False),)),
    )
    qc, kc, vc, qd, kd, vd, zz1 = _in_proj(x1, None, mod1, w_in1, qg, kg, cos, sin, seg, plan1,
                                           (hw, hw, hw, hw, LANES, LANES, 2 * hw))
    yc = _dense_attn(qc, kc, vc, nq=1, tq=512, tk=DENSE_TK, n_q_tok=SEQ, diff=True,
                     lamv=c_lambda[0].astype(F32), subln=c_subln[0].astype(F32).reshape(1, LANES),
                     lam0=_lambda_init(1))
    yd = _win_attn(qd, kd, vd, d_sinks[0].astype(F32) * LOG2E)
    return _out_proj(x1, None, yc, yd, zz1, w_out1, mod1, final_norm.astype(F32).reshape(1, D_MODEL))
```

```python
import functools
import math

import numpy as np
import jax
import jax.numpy as jnp
from jax import lax
from jax.experimental import pallas as pl
from jax.experimental.pallas import tpu as pltpu

D_MODEL = 1024
BATCH = 4
SEQ = 8192
DEPTH = 2
GRID_W = 64
CTX_LEN = 256
HEAD_DIM = 64
ROPE_THETA = 10000.0
EPS = 1e-6
NEG_INF = -1e30
A_HEADS = 8
B_HEADS = 8
NA_ROWS = 8
NA_COLS = 16
C_HEADS = 4
D_HEADS = 8
D_WINDOW = 128
SCALE = HEAD_DIM ** -0.5
LOG2E = math.log2(math.e)
QSCALE = SCALE * LOG2E

LANES = 128
HALF = HEAD_DIM
TOK = SEQ + CTX_LEN
TM = 256
N_LAT_TM = SEQ // TM
VT = 128
QB = 256
DENSE_TK = 1024
TQ_BLK = 128
N_LAT_TQ = SEQ // TQ_BLK
N_CTX_TQ = CTX_LEN // TQ_BLK
GRID_ROWS = SEQ // GRID_W
NB_ROWS = 10
NB_KEYS = NB_ROWS * GRID_W
NB_BLKS = NB_KEYS // TQ_BLK
WIN_BLKS = 1 + 2 * (D_WINDOW // TQ_BLK)
VMEM_LIMIT = 56 * 1024 * 1024

HEAD_PERM = (0, 4, 1, 5, 2, 6, 3, 7)
F32 = jnp.float32
BF16 = jnp.bfloat16


def _lambda_init(layer):
    return 0.8 - 0.6 * math.exp(-0.3 * layer)


def _lane_iota(rows):
    return lax.broadcasted_iota(jnp.int32, (rows, LANES), 1)


def _silu(z):
    return z / (1.0 + jnp.exp(-z))


def _mod_kernel(c_ref, w_ref, b_ref, o_ref):
    o_ref[0] = jnp.dot(_silu(c_ref[...]), w_ref[0], preferred_element_type=F32) + b_ref[0]


def _modulation(cs, w_mod, b_mod):
    tn = D_MODEL
    return pl.pallas_call(
        _mod_kernel,
        out_shape=jax.ShapeDtypeStruct((DEPTH, 8, 3 * D_MODEL), F32),
        grid=(DEPTH, 3 * D_MODEL // tn),
        in_specs=[
            pl.BlockSpec((8, D_MODEL), lambda l, j: (0, 0)),
            pl.BlockSpec((1, D_MODEL, tn), lambda l, j: (l, 0, j)),
            pl.BlockSpec((1, 1, tn), lambda l, j: (l, 0, j)),
        ],
        out_specs=pl.BlockSpec((1, 8, tn), lambda l, j: (l, 0, j)),
        compiler_params=pltpu.CompilerParams(
            dimension_semantics=("arbitrary", "arbitrary"), vmem_limit_bytes=VMEM_LIMIT),
        name="modulation",
    )(cs, w_mod, b_mod.reshape(DEPTH, 1, 3 * D_MODEL))


def _swap16(x):
    lane = _lane_iota(x.shape[0])
    fwd = pltpu.roll(x, LANES - 16, 1)
    bwd = pltpu.roll(x, 16, 1)
    return jnp.where((lane & 16) == 0, fwd, bwd)


def _proj_kernel(*refs, plan, dual):
    if dual:
        x_ref, xc_ref = refs[:2]
        refs = refs[2:]
    else:
        x_ref = refs[0]
        refs = refs[1:]
    mod_ref, w_ref, qg_ref, kg_ref, cos_ref, sin_ref, seg_ref = refs[:7]
    out_refs = refs[7:]

    xt = x_ref[0]
    if dual:
        xt = jnp.where(pl.program_id(1) < N_LAT_TM, xt, xc_ref[0])
    ms = jnp.mean(xt * xt, axis=-1, keepdims=True)
    xn = xt * lax.rsqrt(ms + EPS)
    shift = mod_ref[0, :, 0:D_MODEL]
    scale = mod_ref[0, :, D_MODEL:2 * D_MODEL]
    h = (xn * (1.0 + scale) + shift).astype(BF16)

    cos = cos_ref[...]
    sin = sin_ref[...]
    seg = seg_ref[...]
    col = 0
    for width, outs in plan:
        acc = jnp.dot(h, w_ref[:, col:col + width], preferred_element_type=F32)
        col += width
        for out_idx, lo, hi, norm, rope, scale_q, transpose in outs:
            pieces = []
            for c0 in range(lo, hi, LANES):
                blk = acc[:, c0:c0 + LANES]
                if transpose:
                    blk_t = blk.T.astype(BF16)
                    for u in range(TM // VT):
                        out_refs[out_idx][0, (c0 - lo) // LANES, u] = blk_t[:, u * VT:(u + 1) * VT]
                    continue
                if norm:
                    sq = blk * blk
                    sq_hi = sq.astype(BF16)
                    sq_lo = (sq - sq_hi.astype(F32)).astype(BF16)
                    ssum = (jnp.dot(sq_hi, seg, preferred_element_type=F32)
                            + jnp.dot(sq_lo, seg, preferred_element_type=F32))
                    gain = qg_ref[...] if norm == "q" else kg_ref[...]
                    blk = blk * lax.rsqrt(ssum * (1.0 / HEAD_DIM) + EPS) * gain
                if rope:
                    blk = blk * cos + _swap16(blk) * sin
                if scale_q:
                    blk = blk * QSCALE
                pieces.append(blk.astype(BF16))
            if pieces:
                out_refs[out_idx][0] = pieces[0] if len(pieces) == 1 else jnp.concatenate(pieces, axis=-1)


def _in_proj(x, xc, mod, w, qg, kg, cos, sin, seg, plan, out_widths):
    dual = xc is not None
    n_tiles = TOK // TM
    if dual:
        x_specs = [
            pl.BlockSpec((1, TM, D_MODEL), lambda b, i: (b, jnp.minimum(i, N_LAT_TM - 1), 0)),
            pl.BlockSpec((1, CTX_LEN, D_MODEL), lambda b, i: (b, 0, 0)),
        ]
        xs = (x, xc)
    else:
        x_specs = [pl.BlockSpec((1, TM, D_MODEL), lambda b, i: (b, i, 0))]
        xs = (x,)
    n_in = w.shape[1]
    in_specs = x_specs + [
        pl.BlockSpec((1, 1, 3 * D_MODEL), lambda b, i: (jnp.where(i < N_LAT_TM, b, BATCH), 0, 0)),
        pl.BlockSpec((D_MODEL, n_in), lambda b, i: (0, 0)),
        pl.BlockSpec((1, LANES), lambda b, i: (0, 0)),
        pl.BlockSpec((1, LANES), lambda b, i: (0, 0)),
        pl.BlockSpec((TM, LANES), lambda b, i: (i, 0)),
        pl.BlockSpec((TM, LANES), lambda b, i: (i, 0)),
        pl.BlockSpec((LANES, LANES), lambda b, i: (0, 0)),
    ]
    transposed = {o[0] for _, outs in plan for o in outs if o[6]}
    out_shape, out_specs = [], []
    for idx, wd in enumerate(out_widths):
        if idx in transposed:
            out_shape.append(jax.ShapeDtypeStruct((BATCH, wd // LANES, TOK // VT, LANES, VT), BF16))
            out_specs.append(pl.BlockSpec((1, wd // LANES, TM // VT, LANES, VT),
                                          lambda b, i: (b, 0, i, 0, 0)))
        else:
            out_shape.append(jax.ShapeDtypeStruct((BATCH, TOK, wd), BF16))
            out_specs.append(pl.BlockSpec((1, TM, wd), lambda b, i: (b, i, 0)))
    return pl.pallas_call(
        functools.partial(_proj_kernel, plan=plan, dual=dual),
        out_shape=tuple(out_shape),
        grid=(BATCH, n_tiles),
        in_specs=in_specs,
        out_specs=tuple(out_specs),
        compiler_params=pltpu.CompilerParams(
            dimension_semantics=("arbitrary", "arbitrary"), vmem_limit_bytes=VMEM_LIMIT),
        name="in_proj",
    )(*xs, mod, w, qg, kg, cos, sin, seg)


def _out_kernel(*refs, dual, final):
    if dual:
        x_ref, xc_ref = refs[:2]
        refs = refs[2:]
    else:
        x_ref = refs[0]
        refs = refs[1:]
    y1_ref, y2_ref, z_ref, w_ref, mod_ref = refs[:5]
    refs = refs[5:]
    if final:
        fn_ref, o_ref = refs
    else:
        (o_ref,) = refs

    xt = x_ref[0]
    if dual:
        xt = jnp.where(pl.program_id(1) < N_LAT_TM, xt, xc_ref[0])
    y = jnp.concatenate([y1_ref[0], y2_ref[0]], axis=-1).astype(F32)
    g = (y * _silu(z_ref[0].astype(F32))).astype(BF16)
    d = jnp.dot(g, w_ref[...], preferred_element_type=F32)
    gate = mod_ref[0, :, 2 * D_MODEL:3 * D_MODEL]
    xo = xt + gate * d
    if final:
        ms = jnp.mean(xo * xo, axis=-1, keepdims=True)
        xo = xo * lax.rsqrt(ms + EPS) * fn_ref[...]
    o_ref[0] = xo


def _out_proj(x, xc, y1, y2, z, w, mod, final_gain):
    dual = xc is not None
    final = final_gain is not None
    n_tok = SEQ if final else TOK
    half = y1.shape[-1]
    if dual:
        x_specs = [
            pl.BlockSpec((1, TM, D_MODEL), lambda b, i: (b, jnp.minimum(i, N_LAT_TM - 1), 0)),
            pl.BlockSpec((1, CTX_LEN, D_MODEL), lambda b, i: (b, 0, 0)),
        ]
        xs = (x, xc)
    else:
        x_specs = [pl.BlockSpec((1, TM, D_MODEL), lambda b, i: (b, i, 0))]
        xs = (x,)
    in_specs = x_specs + [
        pl.BlockSpec((1, TM, half), lambda b, i: (b, i, 0)),
        pl.BlockSpec((1, TM, half), lambda b, i: (b, i, 0)),
        pl.BlockSpec((1, TM, 2 * half), lambda b, i: (b, i, 0)),
        pl.BlockSpec((2 * half, D_MODEL), lambda b, i: (0, 0)),
        pl.BlockSpec((1, 1, 3 * D_MODEL), lambda b, i: (jnp.where(i < N_LAT_TM, b, BATCH), 0, 0)),
    ]
    args = xs + (y1, y2, z, w, mod)
    if final:
        in_specs.append(pl.BlockSpec((1, D_MODEL), lambda b, i: (0, 0)))
        args = args + (final_gain,)
    return pl.pallas_call(
        functools.partial(_out_kernel, dual=dual, final=final),
        out_shape=jax.ShapeDtypeStruct((BATCH, n_tok, D_MODEL), F32),
        grid=(BATCH, n_tok // TM),
        in_specs=in_specs,
        out_specs=pl.BlockSpec((1, TM, D_MODEL), lambda b, i: (b, i, 0)),
        compiler_params=pltpu.CompilerParams(
            dimension_semantics=("arbitrary", "arbitrary"), vmem_limit_bytes=VMEM_LIMIT),
        name="out_proj",
    )(*args)


def _write_qt(q_ref, qt_scr, nq, tq):
    feat = lax.broadcasted_iota(jnp.int32, (LANES, tq), 0)
    for j in range(nq):
        qt = q_ref[0, :, j * LANES:(j + 1) * LANES].astype(F32).T
        qt_scr[:, (2 * j) * tq:(2 * j + 1) * tq] = jnp.where(feat < HALF, qt, 0.0).astype(BF16)
        qt_scr[:, (2 * j + 1) * tq:(2 * j + 2) * tq] = jnp.where(feat >= HALF, qt, 0.0).astype(BF16)


def _merge_qt(ot, nq, tq):
    feat = lax.broadcasted_iota(jnp.int32, (LANES, tq), 0)
    pieces = []
    for j in range(nq):
        blk = jnp.where(feat < HALF, ot[:, (2 * j) * tq:(2 * j + 1) * tq],
                        ot[:, (2 * j + 1) * tq:(2 * j + 2) * tq])
        pieces.append(blk.T.astype(BF16))
    return pieces[0] if nq == 1 else jnp.concatenate(pieces, axis=-1)


def _dense_kernel(*refs, nq, tq, tk, n_lat_tiles, diff, lam0):
    if diff:
        q_ref, k_ref, vt_ref, lamv_ref, subln_ref, o_ref, qt_scr, m_scr, l_scr, acc_scr, s_scr = refs
    else:
        q_ref, k_ref, vt_ref, o_ref, qt_scr, m_scr, l_scr, acc_scr, s_scr = refs
    n_q = 2 * nq * tq
    _write_qt(q_ref, qt_scr, nq, tq)
    m_scr[...] = jnp.full((1, n_q), -jnp.inf, F32)
    l_scr[...] = jnp.zeros((1, n_q), F32)
    acc_scr[...] = jnp.zeros((LANES, n_q), F32)

    n_blk = n_q // QB
    ctx_chunk = (SEQ, SEQ // VT, CTX_LEN // VT)
    lat_chunks = [(c * tk, c * (tk // VT), tk // VT) for c in range(SEQ // tk)]

    def scores(chunk, n, half):
        start, _, subs = chunk
        kc = k_ref[0, pl.ds(start, subs * VT), :]
        s_scr[half, n, 0:subs * VT, :] = jnp.dot(kc, qt_scr[:, n * QB:(n + 1) * QB],
                                                 preferred_element_type=F32)

    def softmax_pv(chunk, n, half):
        _, tile0, subs = chunk
        cols = slice(n * QB, (n + 1) * QB)
        vtc = jnp.concatenate([vt_ref[0, 0, tile0 + u] for u in range(subs)], axis=1)
        s = s_scr[half, n, 0:subs * VT, :]
        m_prev = m_scr[:, cols]
        m_new = jnp.maximum(m_prev, jnp.max(s, axis=0, keepdims=True))
        alpha = jnp.exp2(m_prev - m_new)
        p = jnp.exp2(s - m_new)
        l_scr[:, cols] = alpha * l_scr[:, cols] + jnp.sum(p, axis=0, keepdims=True)
        acc_scr[:, cols] = alpha * acc_scr[:, cols] + jnp.dot(
            vtc, p.astype(BF16), preferred_element_type=F32)
        m_scr[:, cols] = m_new

    def half_step(cur, half, nxt):
        for n in range(n_blk):
            if nxt is not None:
                scores(nxt, n, 1 - half)
            softmax_pv(cur, n, half)

    def latent_keys():
        half_step(ctx_chunk, 0, lat_chunks[0])
        for c, chunk in enumerate(lat_chunks):
            half_step(chunk, (c + 1) % 2, lat_chunks[c + 1] if c + 1 < len(lat_chunks) else None)

    def context_keys_only():
        half_step(ctx_chunk, 0, None)

    for n in range(n_blk):
        scores(ctx_chunk, n, 0)
    is_latent = pl.program_id(2) < n_lat_tiles
    pl.when(is_latent)(latent_keys)
    pl.when(jnp.logical_not(is_latent))(context_keys_only)

    ot = acc_scr[...] * (1.0 / l_scr[...])
    if diff:
        lv = lamv_ref[...]
        lam = (jnp.exp(jnp.sum(lv[0:1] * lv[1:2], axis=-1, keepdims=True))
               - jnp.exp(jnp.sum(lv[2:3] * lv[3:4], axis=-1, keepdims=True)) + lam0)
        dlt = (ot[:, 0:tq] - lam * ot[:, tq:2 * tq]).T
        ms = jnp.mean(dlt * dlt, axis=-1, keepdims=True)
        y = dlt * lax.rsqrt(ms + EPS) * subln_ref[...] * (1.0 - lam0)
        o_ref[0] = y.astype(BF16)
    else:
        o_ref[0] = _merge_qt(ot, nq, tq)


def _dense_attn(q, k, vt, *, nq, tq, tk, n_q_tok, diff=False, lamv=None, subln=None, lam0=0.0):
    groups = q.shape[-1] // (nq * LANES)
    n_tiles = n_q_tok // tq
    n_lat_tiles = SEQ // tq
    n_q = 2 * nq * tq
    in_specs = [
        pl.BlockSpec((1, tq, nq * LANES), lambda b, g, i: (b, i, g)),
        pl.BlockSpec((1, TOK, LANES), lambda b, g, i: (b, 0, g)),
        pl.BlockSpec((1, 1, TOK // VT, LANES, VT), lambda b, g, i: (b, g, 0, 0, 0)),
    ]
    args = (q, k, vt)
    if diff:
        in_specs += [pl.BlockSpec((4, HEAD_DIM), lambda b, g, i: (0, 0)),
                     pl.BlockSpec((1, LANES), lambda b, g, i: (0, 0))]
        args = args + (lamv, subln)
    return pl.pallas_call(
        functools.partial(_dense_kernel, nq=nq, tq=tq, tk=tk, n_lat_tiles=n_lat_tiles,
                          diff=diff, lam0=lam0),
        out_shape=jax.ShapeDtypeStruct((BATCH, n_q_tok, groups * nq * LANES), BF16),
        grid=(BATCH, groups, n_tiles),
        in_specs=in_specs,
        out_specs=pl.BlockSpec((1, tq, nq * LANES), lambda b, g, i: (b, i, g)),
        scratch_shapes=[pltpu.VMEM((LANES, n_q), BF16),
                        pltpu.VMEM((1, n_q), F32),
                        pltpu.VMEM((1, n_q), F32),
                        pltpu.VMEM((LANES, n_q), F32),
                        pltpu.VMEM((2, n_q // QB, tk, QB), F32)],
        compiler_params=pltpu.CompilerParams(
            dimension_semantics=("arbitrary", "arbitrary", "arbitrary"),
            vmem_limit_bytes=VMEM_LIMIT),
        name="dense_diff_attn" if diff else "dense_gqa_attn",
    )(*args)


def _nb_kernel(q_ref, k_ref, vt_ref, bias_ref, o_ref, qt_scr, s_scr):
    tq = TQ_BLK
    nq = B_HEADS // 2
    n_keys = NB_KEYS + CTX_LEN
    first = jnp.clip(pl.program_id(1) - 2, 0, N_LAT_TQ - NB_BLKS)
    start = pl.multiple_of(first * tq, tq)
    _write_qt(q_ref, qt_scr, nq, tq)
    for j in range(nq):
        cs = slice(j * LANES, (j + 1) * LANES)
        kcat = jnp.concatenate([k_ref[0, pl.ds(start, NB_KEYS), cs], k_ref[0, SEQ:TOK, cs]], axis=0)
        s_scr[j] = jnp.dot(kcat, qt_scr[:, (2 * j) * tq:(2 * j + 2) * tq], preferred_element_type=F32)
    outs = []
    for j in range(nq):
        vtcat = jnp.concatenate([vt_ref[0, j, first + u] for u in range(NB_BLKS)]
                                + [vt_ref[0, j, SEQ // VT + u] for u in range(CTX_LEN // VT)],
                                axis=1)
        s_n = s_scr[j, 0:NB_KEYS, :] + bias_ref[0, j]
        s_c = s_scr[j, NB_KEYS:n_keys, :]
        m = jnp.maximum(jnp.max(s_n, axis=0, keepdims=True), jnp.max(s_c, axis=0, keepdims=True))
        p = jnp.concatenate([jnp.exp2(s_n - m), jnp.exp2(s_c - m)], axis=0)
        l = jnp.sum(p, axis=0, keepdims=True)
        outs.append(jnp.dot(vtcat, p.astype(BF16), preferred_element_type=F32) * (1.0 / l))
    o_ref[0] = _merge_qt(jnp.concatenate(outs, axis=1), nq, tq)


def _nb_case(i):
    last = N_LAT_TQ - 1
    return jnp.where(i == 0, 0, jnp.where(i == 1, 1, jnp.where(
        i == last - 1, 3, jnp.where(i == last, 4, jnp.where(i > last, 5, 2)))))


def _nb_attn(q, k, vt, bias):
    width = B_HEADS * HEAD_DIM
    n_pairs = B_HEADS // 2
    n_tiles = N_LAT_TQ + N_CTX_TQ

    in_specs = [pl.BlockSpec((1, TQ_BLK, width), lambda b, i: (b, i, 0)),
                pl.BlockSpec((1, TOK, width), lambda b, i: (b, 0, 0), pipeline_mode=pl.Buffered(1)),
                pl.BlockSpec((1, n_pairs, TOK // VT, LANES, VT), lambda b, i: (b, 0, 0, 0, 0),
                             pipeline_mode=pl.Buffered(1)),
                pl.BlockSpec((1, n_pairs, NB_KEYS, 2 * TQ_BLK), lambda b, i: (_nb_case(i), 0, 0, 0))]
    return pl.pallas_call(
        _nb_kernel,
        out_shape=jax.ShapeDtypeStruct((BATCH, TOK, width), BF16),
        grid=(BATCH, n_tiles),
        in_specs=in_specs,
        out_specs=pl.BlockSpec((1, TQ_BLK, width), lambda b, i: (b, i, 0)),
        scratch_shapes=[pltpu.VMEM((LANES, B_HEADS * TQ_BLK), BF16),
                        pltpu.VMEM((n_pairs, NB_KEYS + CTX_LEN, 2 * TQ_BLK), F32)],
        compiler_params=pltpu.CompilerParams(
            dimension_semantics=("arbitrary", "arbitrary"), vmem_limit_bytes=VMEM_LIMIT),
        name="neighbourhood_attn",
    )(q, k, vt, bias)


def _nb_bias_table(rpb):
    grid_spec = pltpu.PrefetchScalarGridSpec(
        num_scalar_prefetch=1,
        grid=(6, B_HEADS),
        in_specs=[],
        out_specs=pl.BlockSpec((1, 1, NB_KEYS, TQ_BLK), lambda c, h, r: (c, h // 2, 0, h % 2)),
    )
    return pl.pallas_call(
        _nb_bias_kernel,
        out_shape=jax.ShapeDtypeStruct((6, B_HEADS // 2, NB_KEYS, 2 * TQ_BLK), F32),
        grid_spec=grid_spec,
        compiler_params=pltpu.CompilerParams(dimension_semantics=("arbitrary", "arbitrary")),
        name="nb_bias_table",
    )(rpb.astype(F32).reshape(-1))


def _nb_bias_kernel(rpb_ref, o_ref):
    c = pl.program_id(0)
    h = pl.program_id(1)
    last = N_LAT_TQ - 1
    tile = jnp.where(c == 3, last - 1, jnp.where(c == 4, last, jnp.minimum(c, 2)))
    r0 = 2 * tile
    base = jnp.clip(2 * tile - 4, 0, GRID_ROWS - NB_ROWS)
    n_dr = 2 * NA_ROWS - 1
    n_dc = 2 * NA_COLS - 1
    kc = lax.broadcasted_iota(jnp.int32, (GRID_W, LANES), 0)
    lane = lax.broadcasted_iota(jnp.int32, (GRID_W, LANES), 1)
    qc = lane & (GRID_W - 1)
    second = lane >= GRID_W
    dc = kc - qc + (NA_COLS - 1)
    cs = jnp.clip(qc - NA_COLS // 2, 0, GRID_W - NA_COLS)
    col_ok = jnp.where(kc >= cs, jnp.where(kc < cs + NA_COLS, 1, 0), 0)
    rs0 = jnp.clip(r0 - NA_ROWS // 2, 0, GRID_ROWS - NA_ROWS)
    rs1 = jnp.clip(r0 + 1 - NA_ROWS // 2, 0, GRID_ROWS - NA_ROWS)
    for krow in range(NB_ROWS):
        kr = base + krow
        dr = kr - r0 + (NA_ROWS - 1)
        ok0 = jnp.where((kr >= rs0) & (kr < rs0 + NA_ROWS) & (c < 5), 1, 0)
        ok1 = jnp.where((kr >= rs1) & (kr < rs1 + NA_ROWS) & (c < 5), 1, 0)
        off0 = (h * n_dr + jnp.clip(dr, 0, n_dr - 1)) * n_dc
        off1 = (h * n_dr + jnp.clip(dr - 1, 0, n_dr - 1)) * n_dc
        acc = jnp.zeros((GRID_W, LANES), F32)
        for j in range(n_dc):
            val = jnp.where(second, rpb_ref[off1 + j], rpb_ref[off0 + j])
            acc = jnp.where(dc == j, val, acc)
        ok = col_ok * jnp.where(second, ok1, ok0)
        o_ref[0, 0, krow * GRID_W:(krow + 1) * GRID_W, :] = jnp.where(ok > 0, acc * LOG2E, NEG_INF)


def _win_kernel(sink_ref, q_ref, k_ref, vt_ref, o_ref, qt_scr, s_scr):
    tq = TQ_BLK
    nq = D_HEADS // 2
    n_blk = 2 * nq * tq // QB
    i = pl.program_id(1)
    first = jnp.clip(i - D_WINDOW // tq, 0, N_LAT_TQ - WIN_BLKS)
    start = pl.multiple_of(first * tq, tq)
    _write_qt(q_ref, qt_scr, nq, tq)
    kcat = jnp.concatenate([k_ref[0, pl.ds(start, WIN_BLKS * tq), :], k_ref[0, SEQ:TOK, :]],
                           axis=0)
    vtcat = jnp.concatenate([vt_ref[0, 0, first + u] for u in range(WIN_BLKS)]
                            + [vt_ref[0, 0, SEQ // VT + u] for u in range(CTX_LEN // VT)],
                            axis=1)
    for n in range(n_blk):
        s_scr[n] = jnp.dot(kcat, qt_scr[:, n * QB:(n + 1) * QB], preferred_element_type=F32)

    span = WIN_BLKS * tq
    n_keys = span + CTX_LEN
    kk = lax.broadcasted_iota(jnp.int32, (n_keys, QB), 0)
    qq = lax.broadcasted_iota(jnp.int32, (n_keys, QB), 1) & (tq - 1)
    rel = kk - qq + (first - i) * tq
    mask = jnp.where(kk >= span, 0.0, jnp.where(jnp.abs(rel) <= D_WINDOW, 0.0, NEG_INF))

    outs = []
    for n in range(n_blk):
        sinks = jnp.concatenate(
            [jnp.full((1, tq), sink_ref[HEAD_PERM[n * QB // tq + g]], F32) for g in range(QB // tq)], axis=1)
        s = s_scr[n] + mask
        m = jnp.maximum(jnp.max(s, axis=0, keepdims=True), sinks)
        p = jnp.exp2(s - m)
        l = jnp.sum(p, axis=0, keepdims=True) + jnp.exp2(sinks - m)
        outs.append(jnp.dot(vtcat, p.astype(BF16), preferred_element_type=F32) * (1.0 / l))
    o_ref[0] = _merge_qt(jnp.concatenate(outs, axis=1), nq, tq)


def _win_attn(q, k, vt, sinks):
    width = D_HEADS * HEAD_DIM
    n_q = D_HEADS * TQ_BLK
    grid_spec = pltpu.PrefetchScalarGridSpec(
        num_scalar_prefetch=1,
        grid=(BATCH, N_LAT_TQ),
        in_specs=[pl.BlockSpec((1, TQ_BLK, width), lambda b, i, s: (b, i, 0)),
                  pl.BlockSpec((1, TOK, LANES), lambda b, i, s: (b, 0, 0)),
                  pl.BlockSpec((1, 1, TOK // VT, LANES, VT), lambda b, i, s: (b, 0, 0, 0, 0))],
        out_specs=pl.BlockSpec((1, TQ_BLK, width), lambda b, i, s: (b, i, 0)),
        scratch_shapes=[pltpu.VMEM((LANES, n_q), BF16),
                        pltpu.VMEM((n_q // QB, WIN_BLKS * TQ_BLK + CTX_LEN, QB), F32)],
    )
    return pl.pallas_call(
        _win_kernel,
        out_shape=jax.ShapeDtypeStruct((BATCH, SEQ, width), BF16),
        grid_spec=grid_spec,
        compiler_params=pltpu.CompilerParams(
            dimension_semantics=("arbitrary", "arbitrary"), vmem_limit_bytes=VMEM_LIMIT),
        name="window_attn",
    )(sinks, q, k, vt)


def _rope_tables():
    t = jnp.arange(SEQ, dtype=jnp.int32)
    row = (t // GRID_W).astype(F32)
    col = (t % GRID_W).astype(F32)
    quarter = HEAD_DIM // 4
    inv_freq = ROPE_THETA ** (-jnp.arange(quarter, dtype=F32) / quarter)
    ang_r = row[:, None] * inv_freq
    ang_c = col[:, None] * inv_freq
    cr, sr, cc, sc = jnp.cos(ang_r), jnp.sin(ang_r), jnp.cos(ang_c), jnp.sin(ang_c)
    cos = jnp.tile(jnp.concatenate([cr, cr, cc, cc], axis=-1), (1, LANES // HEAD_DIM))
    sin = jnp.tile(jnp.concatenate([-sr, sr, -sc, sc], axis=-1), (1, LANES // HEAD_DIM))
    cos = jnp.concatenate([cos, jnp.ones((CTX_LEN, LANES), F32)], axis=0)
    sin = jnp.concatenate([sin, jnp.zeros((CTX_LEN, LANES), F32)], axis=0)
    return cos, sin


def _head_cols(perm):
    return np.concatenate([np.arange(HEAD_DIM) + HEAD_DIM * h for h in perm])


def _take_static(w, idx, axis):
    breaks = np.flatnonzero(np.diff(idx) != 1) + 1
    runs = np.split(idx, breaks)
    parts = [lax.slice_in_dim(w, int(r[0]), int(r[-1]) + 1, axis=axis) for r in runs]
    return parts[0] if len(parts) == 1 else jnp.concatenate(parts, axis=axis)


def kernel(x, c, ctx, c_ctx, w_mod, b_mod, w_in_even, w_out_even, a_q_norm, a_k_norm, b_rpb,
           w_in_odd, w_out_odd, c_lambda, c_subln, d_sinks, final_norm):
    assert DEPTH == 2 and x.shape == (BATCH, SEQ, D_MODEL) and ctx.shape == (BATCH, CTX_LEN, D_MODEL)
    perm = _head_cols(HEAD_PERM)
    hw = A_HEADS * HEAD_DIM

    cs = jnp.concatenate([c, c_ctx[None], jnp.zeros((8 - BATCH - 1, D_MODEL), F32)], axis=0)
    mod = _modulation(cs, w_mod, b_mod)
    mod0 = mod[0].reshape(8, 1, 3 * D_MODEL)
    mod1 = mod[1].reshape(8, 1, 3 * D_MODEL)

    cos, sin = _rope_tables()
    seg = jnp.asarray(np.kron(np.eye(LANES // HEAD_DIM), np.ones((HEAD_DIM, HEAD_DIM))), BF16)
    tile2 = LANES // HEAD_DIM
    qg = jnp.tile(a_q_norm[0].astype(F32), tile2).reshape(1, LANES)
    kg = jnp.tile(a_k_norm[0].astype(F32), tile2).reshape(1, LANES)

    z0 = 2304
    cols0 = np.concatenate([perm, np.arange(hw, z0), z0 + perm, np.arange(z0 + hw, z0 + 2 * hw)])
    w_in0 = _take_static(w_in_even[0], cols0, 1).astype(BF16)
    w_out0 = _take_static(w_out_even[0], np.concatenate([perm, np.arange(hw, 2 * hw)]), 0).astype(BF16)
    plan0 = (
        (hw, ((0, 0, hw, "q", True, True, False),)),
        (2 * LANES, ((1, 0, LANES, "k", True, False, False),
                     (2, LANES, 2 * LANES, None, False, False, True))),
        (hw, ((3, 0, hw, None, False, True, False),)),
        (hw, ((4, 0, hw, None, False, False, False),)),
        (hw, ((5, 0, hw, None, False, False, True),)),
        (2 * hw, ((6, 0, 2 * hw, None, False, False, False),)),
    )
    qa, ka, va, qb, kb, vb, zz = _in_proj(x, ctx, mod0, w_in0, qg, kg, cos, sin, seg, plan0,
                                          (hw, LANES, LANES, hw, hw, hw, 2 * hw))
    ya = _dense_attn(qa, ka, va, nq=4, tq=256, tk=DENSE_TK, n_q_tok=TOK)
    yb = _nb_attn(qb, kb, vb, _nb_bias_table(b_rpb[0]))
    x1 = _out_proj(x, ctx, ya, yb, zz, w_out0, mod0, None)

    cols1 = np.concatenate([np.arange(0, 3 * hw), 3 * hw + perm, np.arange(4 * hw, z0 + hw), z0 + hw + perm])
    w_in1 = _take_static(w_in_odd[0], cols1, 1).astype(BF16)
    w_out1 = _take_static(w_out_odd[0], np.concatenate([np.arange(hw), hw + perm]), 0).astype(BF16)
    plan1 = (
        (hw, ((0, 0, hw, None, True, True, False),)),
        (hw, ((1, 0, hw, None, True, False, False),)),
        (hw, ((2, 0, hw, None, False, False, True),)),
        (hw, ((3, 0, hw, None, True, True, False),)),
        (2 * LANES, ((4, 0, LANES, None, True, False, False),
                     (5, LANES, 2 * LANES, None, False, False, True))),
        (2 * hw, ((6, 0, 2 * hw, None, False, False, False),)),
    )
    qc, kc, vc, qd, kd, vd, zz1 = _in_proj(x1, None, mod1, w_in1, qg, kg, cos, sin, seg, plan1,
                                           (hw, hw, hw, hw, LANES, LANES, 2 * hw))
    yc = _dense_attn(qc, kc, vc, nq=1, tq=1024, tk=DENSE_TK, n_q_tok=SEQ, diff=True,
                     lamv=c_lambda[0].astype(F32), subln=c_subln[0].astype(F32).reshape(1, LANES),
                     lam0=_lambda_init(1))
    yd = _win_attn(qd, kd, vd, d_sinks[0].astype(F32) * LOG2E)
    return _out_proj(x1, None, yc, yd, zz1, w_out1, mod1, final_norm.astype(F32).reshape(1, D_MODEL))
```

```python
import functools
import math

import numpy as np
import jax
import jax.numpy as jnp
from jax import lax
from jax.experimental import pallas as pl
from jax.experimental.pallas import tpu as pltpu

D_MODEL = 1024
BATCH = 4
SEQ = 8192
DEPTH = 2
GRID_W = 64
CTX_LEN = 256
HEAD_DIM = 64
ROPE_THETA = 10000.0
EPS = 1e-6
NEG_INF = -1e30
A_HEADS = 8
B_HEADS = 8
NA_ROWS = 8
NA_COLS = 16
C_HEADS = 4
D_HEADS = 8
D_WINDOW = 128
SCALE = HEAD_DIM ** -0.5
LOG2E = math.log2(math.e)
QSCALE = SCALE * LOG2E

LANES = 128
HALF = HEAD_DIM
TOK = SEQ + CTX_LEN
TM = 256
N_LAT_TM = SEQ // TM
TM_FINAL = 512
VT = 128
QB = 256
DENSE_TK = 1024
TQ_BLK = 128
N_LAT_TQ = SEQ // TQ_BLK
N_CTX_TQ = CTX_LEN // TQ_BLK
GRID_ROWS = SEQ // GRID_W
NB_ROWS = 10
NB_KEYS = NB_ROWS * GRID_W
NB_BLKS = NB_KEYS // TQ_BLK
WIN_BLKS = 1 + 2 * (D_WINDOW // TQ_BLK)
VMEM_LIMIT = 56 * 1024 * 1024

HEAD_PERM = (0, 4, 1, 5, 2, 6, 3, 7)
F32 = jnp.float32
BF16 = jnp.bfloat16


def _lambda_init(layer):
    return 0.8 - 0.6 * math.exp(-0.3 * layer)


def _lane_iota(rows):
    return lax.broadcasted_iota(jnp.int32, (rows, LANES), 1)


def _silu(z):
    return z / (1.0 + jnp.exp(-z))


def _mod_kernel(c_ref, w_ref, b_ref, o_ref):
    o_ref[0] = jnp.dot(_silu(c_ref[...]), w_ref[0], preferred_element_type=F32) + b_ref[0]


def _modulation(cs, w_mod, b_mod):
    tn = D_MODEL
    return pl.pallas_call(
        _mod_kernel,
        out_shape=jax.ShapeDtypeStruct((DEPTH, 8, 3 * D_MODEL), F32),
        grid=(DEPTH, 3 * D_MODEL // tn),
        in_specs=[
            pl.BlockSpec((8, D_MODEL), lambda l, j: (0, 0)),
            pl.BlockSpec((1, D_MODEL, tn), lambda l, j: (l, 0, j)),
            pl.BlockSpec((1, 1, tn), lambda l, j: (l, 0, j)),
        ],
        out_specs=pl.BlockSpec((1, 8, tn), lambda l, j: (l, 0, j)),
        compiler_params=pltpu.CompilerParams(
            dimension_semantics=("arbitrary", "arbitrary"), vmem_limit_bytes=VMEM_LIMIT),
        name="modulation",
    )(cs, w_mod, b_mod.reshape(DEPTH, 1, 3 * D_MODEL))


def _swap16(x):
    lane = _lane_iota(x.shape[0])
    fwd = pltpu.roll(x, LANES - 16, 1)
    bwd = pltpu.roll(x, 16, 1)
    return jnp.where((lane & 16) == 0, fwd, bwd)


def _proj_kernel(*refs, plan, dual):
    if dual:
        x_ref, xc_ref = refs[:2]
        refs = refs[2:]
    else:
        x_ref = refs[0]
        refs = refs[1:]
    mod_ref, w_ref, qg_ref, kg_ref, cos_ref, sin_ref, seg_ref = refs[:7]
    out_refs = refs[7:]

    xt = x_ref[0]
    if dual:
        xt = jnp.where(pl.program_id(1) < N_LAT_TM, xt, xc_ref[0])
    ms = jnp.mean(xt * xt, axis=-1, keepdims=True)
    xn = xt * lax.rsqrt(ms + EPS)
    shift = mod_ref[0, :, 0:D_MODEL]
    scale = mod_ref[0, :, D_MODEL:2 * D_MODEL]
    h = (xn * (1.0 + scale) + shift).astype(BF16)

    cos = cos_ref[...]
    sin = sin_ref[...]
    seg = seg_ref[...]
    col = 0
    for width, outs in plan:
        acc = jnp.dot(h, w_ref[:, col:col + width], preferred_element_type=F32)
        col += width
        for out_idx, lo, hi, norm, rope, scale_q, transpose in outs:
            pieces = []
            for c0 in range(lo, hi, LANES):
                blk = acc[:, c0:c0 + LANES]
                if norm:
                    sq = blk * blk
                    sq_hi = sq.astype(BF16)
                    sq_lo = (sq - sq_hi.astype(F32)).astype(BF16)
                    ssum = (jnp.dot(sq_hi, seg, preferred_element_type=F32)
                            + jnp.dot(sq_lo, seg, preferred_element_type=F32))
                    gain = qg_ref[...] if norm == "q" else kg_ref[...]
                    blk = blk * lax.rsqrt(ssum * (1.0 / HEAD_DIM) + EPS) * gain
                if rope:
                    blk = blk * cos + _swap16(blk) * sin
                if scale_q:
                    blk = blk * QSCALE
                if transpose:
                    blk_t = blk.T.astype(BF16)
                    for u in range(TM // VT):
                        out_refs[out_idx][0, (c0 - lo) // LANES, u] = blk_t[:, u * VT:(u + 1) * VT]
                    continue
                pieces.append(blk.astype(BF16))
            if pieces:
                out_refs[out_idx][0] = pieces[0] if len(pieces) == 1 else jnp.concatenate(pieces, axis=-1)


def _in_proj(x, xc, mod, w, qg, kg, cos, sin, seg, plan, out_widths):
    dual = xc is not None
    n_tiles = TOK // TM
    if dual:
        x_specs = [
            pl.BlockSpec((1, TM, D_MODEL), lambda b, i: (b, jnp.minimum(i, N_LAT_TM - 1), 0)),
            pl.BlockSpec((1, CTX_LEN, D_MODEL), lambda b, i: (b, 0, 0)),
        ]
        xs = (x, xc)
    else:
        x_specs = [pl.BlockSpec((1, TM, D_MODEL), lambda b, i: (b, i, 0))]
        xs = (x,)
    n_in = w.shape[1]
    in_specs = x_specs + [
        pl.BlockSpec((1, 1, 3 * D_MODEL), lambda b, i: (jnp.where(i < N_LAT_TM, b, BATCH), 0, 0)),
        pl.BlockSpec((D_MODEL, n_in), lambda b, i: (0, 0)),
        pl.BlockSpec((1, LANES), lambda b, i: (0, 0)),
        pl.BlockSpec((1, LANES), lambda b, i: (0, 0)),
        pl.BlockSpec((TM, LANES), lambda b, i: (i, 0)),
        pl.BlockSpec((TM, LANES), lambda b, i: (i, 0)),
        pl.BlockSpec((LANES, LANES), lambda b, i: (0, 0)),
    ]
    transposed = {o[0] for _, outs in plan for o in outs if o[6]}
    out_shape, out_specs = [], []
    for idx, wd in enumerate(out_widths):
        if idx in transposed:
            out_shape.append(jax.ShapeDtypeStruct((BATCH, wd // LANES, TOK // VT, LANES, VT), BF16))
            out_specs.append(pl.BlockSpec((1, wd // LANES, TM // VT, LANES, VT),
                                          lambda b, i: (b, 0, i, 0, 0)))
        else:
            out_shape.append(jax.ShapeDtypeStruct((BATCH, TOK, wd), BF16))
            out_specs.append(pl.BlockSpec((1, TM, wd), lambda b, i: (b, i, 0)))
    return pl.pallas_call(
        functools.partial(_proj_kernel, plan=plan, dual=dual),
        out_shape=tuple(out_shape),
        grid=(BATCH, n_tiles),
        in_specs=in_specs,
        out_specs=tuple(out_specs),
        compiler_params=pltpu.CompilerParams(
            dimension_semantics=("arbitrary", "arbitrary"), vmem_limit_bytes=VMEM_LIMIT),
        name="in_proj",
    )(*xs, mod, w, qg, kg, cos, sin, seg)


def _out_kernel(*refs, dual, final):
    if dual:
        x_ref, xc_ref = refs[:2]
        refs = refs[2:]
    else:
        x_ref = refs[0]
        refs = refs[1:]
    y1_ref, y2_ref, z_ref, w_ref, mod_ref = refs[:5]
    refs = refs[5:]
    if final:
        fn_ref, o_ref = refs
    else:
        (o_ref,) = refs

    xt = x_ref[0]
    if dual:
        xt = jnp.where(pl.program_id(1) < N_LAT_TM, xt, xc_ref[0])
    y = jnp.concatenate([y1_ref[0], y2_ref[0]], axis=-1).astype(F32)
    g = (y * _silu(z_ref[0].astype(F32))).astype(BF16)
    d = jnp.dot(g, w_ref[...], preferred_element_type=F32)
    gate = mod_ref[0, :, 2 * D_MODEL:3 * D_MODEL]
    xo = xt + gate * d
    if final:
        ms = jnp.mean(xo * xo, axis=-1, keepdims=True)
        xo = xo * lax.rsqrt(ms + EPS) * fn_ref[...]
    o_ref[0] = xo


def _out_proj(x, xc, y1, y2, z, w, mod, final_gain):
    dual = xc is not None
    final = final_gain is not None
    n_tok = SEQ if final else TOK
    tm = TM_FINAL if final else TM
    lat_tiles = SEQ // tm
    half = y1.shape[-1]
    if dual:
        x_specs = [
            pl.BlockSpec((1, tm, D_MODEL), lambda b, i: (b, jnp.minimum(i, lat_tiles - 1), 0)),
            pl.BlockSpec((1, CTX_LEN, D_MODEL), lambda b, i: (b, 0, 0)),
        ]
        xs = (x, xc)
    else:
        x_specs = [pl.BlockSpec((1, tm, D_MODEL), lambda b, i: (b, i, 0))]
        xs = (x,)
    in_specs = x_specs + [
        pl.BlockSpec((1, tm, half), lambda b, i: (b, i, 0)),
        pl.BlockSpec((1, tm, half), lambda b, i: (b, i, 0)),
        pl.BlockSpec((1, tm, 2 * half), lambda b, i: (b, i, 0)),
        pl.BlockSpec((2 * half, D_MODEL), lambda b, i: (0, 0)),
        pl.BlockSpec((1, 1, 3 * D_MODEL), lambda b, i: (jnp.where(i < lat_tiles, b, BATCH), 0, 0)),
    ]
    args = xs + (y1, y2, z, w, mod)
    if final:
        in_specs.append(pl.BlockSpec((1, D_MODEL), lambda b, i: (0, 0)))
        args = args + (final_gain,)
    return pl.pallas_call(
        functools.partial(_out_kernel, dual=dual, final=final),
        out_shape=jax.ShapeDtypeStruct((BATCH, n_tok, D_MODEL), F32),
        grid=(BATCH, n_tok // tm),
        in_specs=in_specs,
        out_specs=pl.BlockSpec((1, tm, D_MODEL), lambda b, i: (b, i, 0)),
        compiler_params=pltpu.CompilerParams(
            dimension_semantics=("arbitrary", "arbitrary"), vmem_limit_bytes=VMEM_LIMIT),
        name="out_proj",
    )(*args)


def _write_qt(q_ref, qt_scr, nq, tq):
    feat = lax.broadcasted_iota(jnp.int32, (LANES, VT), 0)
    for j in range(nq):
        for u in range(tq // VT):
            qt = q_ref[0, j, u].astype(F32)
            lo = (2 * j) * tq + u * VT
            hi = (2 * j + 1) * tq + u * VT
            qt_scr[:, lo:lo + VT] = jnp.where(feat < HALF, qt, 0.0).astype(BF16)
            qt_scr[:, hi:hi + VT] = jnp.where(feat >= HALF, qt, 0.0).astype(BF16)


def _qt_spec(nq, tq, index_map):
    return pl.BlockSpec((1, nq, tq // VT, LANES, VT), index_map)


def _merge_qt(ot, nq, tq):
    feat = lax.broadcasted_iota(jnp.int32, (LANES, tq), 0)
    pieces = []
    for j in range(nq):
        blk = jnp.where(feat < HALF, ot[:, (2 * j) * tq:(2 * j + 1) * tq],
                        ot[:, (2 * j + 1) * tq:(2 * j + 2) * tq])
        pieces.append(blk.T.astype(BF16))
    return pieces[0] if nq == 1 else jnp.concatenate(pieces, axis=-1)


def _dense_kernel(*refs, nq, tq, tk, n_lat_tiles, diff, lam0):
    if diff:
        q_ref, k_ref, vt_ref, lamv_ref, subln_ref, o_ref, qt_scr, m_scr, l_scr, acc_scr, s_scr = refs
    else:
        q_ref, k_ref, vt_ref, o_ref, qt_scr, m_scr, l_scr, acc_scr, s_scr = refs
    n_q = 2 * nq * tq
    _write_qt(q_ref, qt_scr, nq, tq)
    m_scr[...] = jnp.full((1, n_q), -jnp.inf, F32)
    l_scr[...] = jnp.zeros((1, n_q), F32)
    acc_scr[...] = jnp.zeros((LANES, n_q), F32)

    n_blk = n_q // QB
    ctx_chunk = (SEQ, SEQ // VT, CTX_LEN // VT)
    lat_chunks = [(c * tk, c * (tk // VT), tk // VT) for c in range(SEQ // tk)]

    def scores(chunk, n, half):
        start, _, subs = chunk
        kc = k_ref[0, pl.ds(start, subs * VT), :]
        s_scr[half, n, 0:subs * VT, :] = jnp.dot(kc, qt_scr[:, n * QB:(n + 1) * QB],
                                                 preferred_element_type=F32)

    def softmax_pv(chunk, n, half):
        _, tile0, subs = chunk
        cols = slice(n * QB, (n + 1) * QB)
        vtc = jnp.concatenate([vt_ref[0, 0, tile0 + u] for u in range(subs)], axis=1)
        s = s_scr[half, n, 0:subs * VT, :]
        m_prev = m_scr[:, cols]
        m_new = jnp.maximum(m_prev, jnp.max(s, axis=0, keepdims=True))
        alpha = jnp.exp2(m_prev - m_new)
        p = jnp.exp2(s - m_new)
        l_scr[:, cols] = alpha * l_scr[:, cols] + jnp.sum(p, axis=0, keepdims=True)
        acc_scr[:, cols] = alpha * acc_scr[:, cols] + jnp.dot(
            vtc, p.astype(BF16), preferred_element_type=F32)
        m_scr[:, cols] = m_new

    def half_step(cur, half, nxt):
        for n in range(n_blk):
            if nxt is not None:
                scores(nxt, n, 1 - half)
            softmax_pv(cur, n, half)

    def latent_keys():
        half_step(ctx_chunk, 0, lat_chunks[0])
        for c, chunk in enumerate(lat_chunks):
            half_step(chunk, (c + 1) % 2, lat_chunks[c + 1] if c + 1 < len(lat_chunks) else None)

    def context_keys_only():
        half_step(ctx_chunk, 0, None)

    for n in range(n_blk):
        scores(ctx_chunk, n, 0)
    is_latent = pl.program_id(2) < n_lat_tiles
    pl.when(is_latent)(latent_keys)
    pl.when(jnp.logical_not(is_latent))(context_keys_only)

    ot = acc_scr[...] * (1.0 / l_scr[...])
    if diff:
        lv = lamv_ref[...]
        lam = (jnp.exp(jnp.sum(lv[0:1] * lv[1:2], axis=-1, keepdims=True))
               - jnp.exp(jnp.sum(lv[2:3] * lv[3:4], axis=-1, keepdims=True)) + lam0)
        dlt = (ot[:, 0:tq] - lam * ot[:, tq:2 * tq]).T
        ms = jnp.mean(dlt * dlt, axis=-1, keepdims=True)
        y = dlt * lax.rsqrt(ms + EPS) * subln_ref[...] * (1.0 - lam0)
        o_ref[0] = y.astype(BF16)
    else:
        o_ref[0] = _merge_qt(ot, nq, tq)


def _dense_attn(q, k, vt, *, nq, tq, tk, n_q_tok, diff=False, lamv=None, subln=None, lam0=0.0):
    groups = q.shape[1] // nq
    n_tiles = n_q_tok // tq
    n_lat_tiles = SEQ // tq
    n_q = 2 * nq * tq
    in_specs = [
        _qt_spec(nq, tq, lambda b, g, i: (b, g, i, 0, 0)),
        pl.BlockSpec((1, TOK, LANES), lambda b, g, i: (b, 0, g)),
        pl.BlockSpec((1, 1, TOK // VT, LANES, VT), lambda b, g, i: (b, g, 0, 0, 0)),
    ]
    args = (q, k, vt)
    if diff:
        in_specs += [pl.BlockSpec((4, HEAD_DIM), lambda b, g, i: (0, 0)),
                     pl.BlockSpec((1, LANES), lambda b, g, i: (0, 0))]
        args = args + (lamv, subln)
    return pl.pallas_call(
        functools.partial(_dense_kernel, nq=nq, tq=tq, tk=tk, n_lat_tiles=n_lat_tiles,
                          diff=diff, lam0=lam0),
        out_shape=jax.ShapeDtypeStruct((BATCH, n_q_tok, groups * nq * LANES), BF16),
        grid=(BATCH, groups, n_tiles),
        in_specs=in_specs,
        out_specs=pl.BlockSpec((1, tq, nq * LANES), lambda b, g, i: (b, i, g)),
        scratch_shapes=[pltpu.VMEM((LANES, n_q), BF16),
                        pltpu.VMEM((1, n_q), F32),
                        pltpu.VMEM((1, n_q), F32),
                        pltpu.VMEM((LANES, n_q), F32),
                        pltpu.VMEM((2, n_q // QB, tk, QB), F32)],
        compiler_params=pltpu.CompilerParams(
            dimension_semantics=("arbitrary", "arbitrary", "arbitrary"),
            vmem_limit_bytes=VMEM_LIMIT),
        name="dense_diff_attn" if diff else "dense_gqa_attn",
    )(*args)


def _nb_kernel(q_ref, k_ref, vt_ref, bias_ref, o_ref, qt_scr, s_scr):
    tq = TQ_BLK
    nq = B_HEADS // 2
    n_keys = NB_KEYS + CTX_LEN
    first = jnp.clip(pl.program_id(1) - 2, 0, N_LAT_TQ - NB_BLKS)
    start = pl.multiple_of(first * tq, tq)
    _write_qt(q_ref, qt_scr, nq, tq)
    for j in range(nq):
        cs = slice(j * LANES, (j + 1) * LANES)
        kcat = jnp.concatenate([k_ref[0, pl.ds(start, NB_KEYS), cs], k_ref[0, SEQ:TOK, cs]], axis=0)
        s_scr[j] = jnp.dot(kcat, qt_scr[:, (2 * j) * tq:(2 * j + 2) * tq], preferred_element_type=F32)
    outs = []
    for j in range(nq):
        vtcat = jnp.concatenate([vt_ref[0, j, first + u] for u in range(NB_BLKS)]
                                + [vt_ref[0, j, SEQ // VT + u] for u in range(CTX_LEN // VT)],
                                axis=1)
        s_n = s_scr[j, 0:NB_KEYS, :] + bias_ref[0, j]
        s_c = s_scr[j, NB_KEYS:n_keys, :]
        m = jnp.maximum(jnp.max(s_n, axis=0, keepdims=True), jnp.max(s_c, axis=0, keepdims=True))
        p = jnp.concatenate([jnp.exp2(s_n - m), jnp.exp2(s_c - m)], axis=0)
        l = jnp.sum(p, axis=0, keepdims=True)
        outs.append(jnp.dot(vtcat, p.astype(BF16), preferred_element_type=F32) * (1.0 / l))
    o_ref[0] = _merge_qt(jnp.concatenate(outs, axis=1), nq, tq)


def _nb_case(i):
    last = N_LAT_TQ - 1
    return jnp.where(i == 0, 0, jnp.where(i == 1, 1, jnp.where(
        i == last - 1, 3, jnp.where(i == last, 4, jnp.where(i > last, 5, 2)))))


def _nb_attn(q, k, vt, bias):
    width = B_HEADS * HEAD_DIM
    n_pairs = B_HEADS // 2
    n_tiles = N_LAT_TQ + N_CTX_TQ

    in_specs = [_qt_spec(n_pairs, TQ_BLK, lambda b, i: (b, 0, i, 0, 0)),
                pl.BlockSpec((1, TOK, width), lambda b, i: (b, 0, 0), pipeline_mode=pl.Buffered(1)),
                pl.BlockSpec((1, n_pairs, TOK // VT, LANES, VT), lambda b, i: (b, 0, 0, 0, 0),
                             pipeline_mode=pl.Buffered(1)),
                pl.BlockSpec((1, n_pairs, NB_KEYS, 2 * TQ_BLK), lambda b, i: (_nb_case(i), 0, 0, 0))]
    return pl.pallas_call(
        _nb_kernel,
        out_shape=jax.ShapeDtypeStruct((BATCH, TOK, width), BF16),
        grid=(BATCH, n_tiles),
        in_specs=in_specs,
        out_specs=pl.BlockSpec((1, TQ_BLK, width), lambda b, i: (b, i, 0)),
        scratch_shapes=[pltpu.VMEM((LANES, B_HEADS * TQ_BLK), BF16),
                        pltpu.VMEM((n_pairs, NB_KEYS + CTX_LEN, 2 * TQ_BLK), F32)],
        compiler_params=pltpu.CompilerParams(
            dimension_semantics=("arbitrary", "arbitrary"), vmem_limit_bytes=VMEM_LIMIT),
        name="neighbourhood_attn",
    )(q, k, vt, bias)


def _nb_bias_table(rpb):
    grid_spec = pltpu.PrefetchScalarGridSpec(
        num_scalar_prefetch=1,
        grid=(6, B_HEADS),
        in_specs=[],
        out_specs=pl.BlockSpec((1, 1, NB_KEYS, TQ_BLK), lambda c, h, r: (c, h // 2, 0, h % 2)),
    )
    return pl.pallas_call(
        _nb_bias_kernel,
        out_shape=jax.ShapeDtypeStruct((6, B_HEADS // 2, NB_KEYS, 2 * TQ_BLK), F32),
        grid_spec=grid_spec,
        compiler_params=pltpu.CompilerParams(dimension_semantics=("arbitrary", "arbitrary")),
        name="nb_bias_table",
    )(rpb.astype(F32).reshape(-1))


def _nb_bias_kernel(rpb_ref, o_ref):
    c = pl.program_id(0)
    h = pl.program_id(1)
    last = N_LAT_TQ - 1
    tile = jnp.where(c == 3, last - 1, jnp.where(c == 4, last, jnp.minimum(c, 2)))
    r0 = 2 * tile
    base = jnp.clip(2 * tile - 4, 0, GRID_ROWS - NB_ROWS)
    n_dr = 2 * NA_ROWS - 1
    n_dc = 2 * NA_COLS - 1
    kc = lax.broadcasted_iota(jnp.int32, (GRID_W, LANES), 0)
    lane = lax.broadcasted_iota(jnp.int32, (GRID_W, LANES), 1)
    qc = lane & (GRID_W - 1)
    second = lane >= GRID_W
    dc = kc - qc + (NA_COLS - 1)
    cs = jnp.clip(qc - NA_COLS // 2, 0, GRID_W - NA_COLS)
    col_ok = jnp.where(kc >= cs, jnp.where(kc < cs + NA_COLS, 1, 0), 0)
    rs0 = jnp.clip(r0 - NA_ROWS // 2, 0, GRID_ROWS - NA_ROWS)
    rs1 = jnp.clip(r0 + 1 - NA_ROWS // 2, 0, GRID_ROWS - NA_ROWS)
    for krow in range(NB_ROWS):
        kr = base + krow
        dr = kr - r0 + (NA_ROWS - 1)
        ok0 = jnp.where((kr >= rs0) & (kr < rs0 + NA_ROWS) & (c < 5), 1, 0)
        ok1 = jnp.where((kr >= rs1) & (kr < rs1 + NA_ROWS) & (c < 5), 1, 0)
        off0 = (h * n_dr + jnp.clip(dr, 0, n_dr - 1)) * n_dc
        off1 = (h * n_dr + jnp.clip(dr - 1, 0, n_dr - 1)) * n_dc
        acc = jnp.zeros((GRID_W, LANES), F32)
        for j in range(n_dc):
            val = jnp.where(second, rpb_ref[off1 + j], rpb_ref[off0 + j])
            acc = jnp.where(dc == j, val, acc)
        ok = col_ok * jnp.where(second, ok1, ok0)
        o_ref[0, 0, krow * GRID_W:(krow + 1) * GRID_W, :] = jnp.where(ok > 0, acc * LOG2E, NEG_INF)


def _win_kernel(sink_ref, q_ref, k_ref, vt_ref, o_ref, qt_scr, s_scr):
    tq = TQ_BLK
    nq = D_HEADS // 2
    n_blk = 2 * nq * tq // QB
    i = pl.program_id(1)
    first = jnp.clip(i - D_WINDOW // tq, 0, N_LAT_TQ - WIN_BLKS)
    start = pl.multiple_of(first * tq, tq)
    _write_qt(q_ref, qt_scr, nq, tq)
    kcat = jnp.concatenate([k_ref[0, pl.ds(start, WIN_BLKS * tq), :], k_ref[0, SEQ:TOK, :]],
                           axis=0)
    vtcat = jnp.concatenate([vt_ref[0, 0, first + u] for u in range(WIN_BLKS)]
                            + [vt_ref[0, 0, SEQ // VT + u] for u in range(CTX_LEN // VT)],
                            axis=1)
    for n in range(n_blk):
        s_scr[n] = jnp.dot(kcat, qt_scr[:, n * QB:(n + 1) * QB], preferred_element_type=F32)

    span = WIN_BLKS * tq
    n_keys = span + CTX_LEN
    kk = lax.broadcasted_iota(jnp.int32, (n_keys, QB), 0)
    qq = lax.broadcasted_iota(jnp.int32, (n_keys, QB), 1) & (tq - 1)
    rel = kk - qq + (first - i) * tq
    mask = jnp.where(kk >= span, 0.0, jnp.where(jnp.abs(rel) <= D_WINDOW, 0.0, NEG_INF))

    outs = []
    for n in range(n_blk):
        sinks = jnp.concatenate(
            [jnp.full((1, tq), sink_ref[HEAD_PERM[n * QB // tq + g]], F32) for g in range(QB // tq)], axis=1)
        s = s_scr[n] + mask
        m = jnp.maximum(jnp.max(s, axis=0, keepdims=True), sinks)
        p = jnp.exp2(s - m)
        l = jnp.sum(p, axis=0, keepdims=True) + jnp.exp2(sinks - m)
        outs.append(jnp.dot(vtcat, p.astype(BF16), preferred_element_type=F32) * (1.0 / l))
    o_ref[0] = _merge_qt(jnp.concatenate(outs, axis=1), nq, tq)


def _win_attn(q, k, vt, sinks):
    width = D_HEADS * HEAD_DIM
    n_q = D_HEADS * TQ_BLK
    grid_spec = pltpu.PrefetchScalarGridSpec(
        num_scalar_prefetch=1,
        grid=(BATCH, N_LAT_TQ),
        in_specs=[_qt_spec(D_HEADS // 2, TQ_BLK, lambda b, i, s: (b, 0, i, 0, 0)),
                  pl.BlockSpec((1, TOK, LANES), lambda b, i, s: (b, 0, 0)),
                  pl.BlockSpec((1, 1, TOK // VT, LANES, VT), lambda b, i, s: (b, 0, 0, 0, 0))],
        out_specs=pl.BlockSpec((1, TQ_BLK, width), lambda b, i, s: (b, i, 0)),
        scratch_shapes=[pltpu.VMEM((LANES, n_q), BF16),
                        pltpu.VMEM((n_q // QB, WIN_BLKS * TQ_BLK + CTX_LEN, QB), F32)],
    )
    return pl.pallas_call(
        _win_kernel,
        out_shape=jax.ShapeDtypeStruct((BATCH, SEQ, width), BF16),
        grid_spec=grid_spec,
        compiler_params=pltpu.CompilerParams(
            dimension_semantics=("arbitrary", "arbitrary"), vmem_limit_bytes=VMEM_LIMIT),
        name="window_attn",
    )(sinks, q, k, vt)


def _rope_tables():
    t = jnp.arange(SEQ, dtype=jnp.int32)
    row = (t // GRID_W).astype(F32)
    col = (t % GRID_W).astype(F32)
    quarter = HEAD_DIM // 4
    inv_freq = ROPE_THETA ** (-jnp.arange(quarter, dtype=F32) / quarter)
    ang_r = row[:, None] * inv_freq
    ang_c = col[:, None] * inv_freq
    cr, sr, cc, sc = jnp.cos(ang_r), jnp.sin(ang_r), jnp.cos(ang_c), jnp.sin(ang_c)
    cos = jnp.tile(jnp.concatenate([cr, cr, cc, cc], axis=-1), (1, LANES // HEAD_DIM))
    sin = jnp.tile(jnp.concatenate([-sr, sr, -sc, sc], axis=-1), (1, LANES // HEAD_DIM))
    cos = jnp.concatenate([cos, jnp.ones((CTX_LEN, LANES), F32)], axis=0)
    sin = jnp.concatenate([sin, jnp.zeros((CTX_LEN, LANES), F32)], axis=0)
    return cos, sin


def _head_cols(perm):
    return np.concatenate([np.arange(HEAD_DIM) + HEAD_DIM * h for h in perm])


def _take_static(w, idx, axis):
    breaks = np.flatnonzero(np.diff(idx) != 1) + 1
    runs = np.split(idx, breaks)
    parts = [lax.slice_in_dim(w, int(r[0]), int(r[-1]) + 1, axis=axis) for r in runs]
    return parts[0] if len(parts) == 1 else jnp.concatenate(parts, axis=axis)


def kernel(x, c, ctx, c_ctx, w_mod, b_mod, w_in_even, w_out_even, a_q_norm, a_k_norm, b_rpb,
           w_in_odd, w_out_odd, c_lambda, c_subln, d_sinks, final_norm):
    assert DEPTH == 2 and x.shape == (BATCH, SEQ, D_MODEL) and ctx.shape == (BATCH, CTX_LEN, D_MODEL)
    perm = _head_cols(HEAD_PERM)
    hw = A_HEADS * HEAD_DIM

    cs = jnp.concatenate([c, c_ctx[None], jnp.zeros((8 - BATCH - 1, D_MODEL), F32)], axis=0)
    mod = _modulation(cs, w_mod, b_mod)
    mod0 = mod[0].reshape(8, 1, 3 * D_MODEL)
    mod1 = mod[1].reshape(8, 1, 3 * D_MODEL)

    cos, sin = _rope_tables()
    seg = jnp.asarray(np.kron(np.eye(LANES // HEAD_DIM), np.ones((HEAD_DIM, HEAD_DIM))), BF16)
    tile2 = LANES // HEAD_DIM
    qg = jnp.tile(a_q_norm[0].astype(F32), tile2).reshape(1, LANES)
    kg = jnp.tile(a_k_norm[0].astype(F32), tile2).reshape(1, LANES)

    z0 = 2304
    cols0 = np.concatenate([perm, np.arange(hw, z0), z0 + perm, np.arange(z0 + hw, z0 + 2 * hw)])
    w_in0 = _take_static(w_in_even[0], cols0, 1).astype(BF16)
    w_out0 = _take_static(w_out_even[0], np.concatenate([perm, np.arange(hw, 2 * hw)]), 0).astype(BF16)
    plan0 = (
        (hw, ((0, 0, hw, "q", True, True, True),)),
        (2 * LANES, ((1, 0, LANES, "k", True, False, False),
                     (2, LANES, 2 * LANES, None, False, False, True))),
        (hw, ((3, 0, hw, None, False, True, True),)),
        (hw, ((4, 0, hw, None, False, False, False),)),
        (hw, ((5, 0, hw, None, False, False, True),)),
        (2 * hw, ((6, 0, 2 * hw, None, False, False, False),)),
    )
    qa, ka, va, qb, kb, vb, zz = _in_proj(x, ctx, mod0, w_in0, qg, kg, cos, sin, seg, plan0,
                                          (hw, LANES, LANES, hw, hw, hw, 2 * hw))
    ya = _dense_attn(qa, ka, va, nq=4, tq=256, tk=DENSE_TK, n_q_tok=TOK)
    yb = _nb_attn(qb, kb, vb, _nb_bias_table(b_rpb[0]))
    x1 = _out_proj(x, ctx, ya, yb, zz, w_out0, mod0, None)

    cols1 = np.concatenate([np.arange(0, 3 * hw), 3 * hw + perm, np.arange(4 * hw, z0 + hw), z0 + hw + perm])
    w_in1 = _take_static(w_in_odd[0], cols1, 1).astype(BF16)
    w_out1 = _take_static(w_out_odd[0], np.concatenate([np.arange(hw), hw + perm]), 0).astype(BF16)
    plan1 = (
        (hw, ((0, 0, hw, None, True, True, True),)),
        (hw, ((1, 0, hw, None, True, False, False),)),
        (hw, ((2, 0, hw, None, False, False, True),)),
        (hw, ((3, 0, hw, None, True, True, True),)),
        (2 * LANES, ((4, 0, LANES, None, True, False, False),
                     (5, LANES, 2 * LANES, None, False, False, True))),
        (2 * hw, ((6, 0, 2 * hw, None, False, False, False),)),
    )
    qc, kc, vc, qd, kd, vd, zz1 = _in_proj(x1, None, mod1, w_in1, qg, kg, cos, sin, seg, plan1,
                                           (hw, hw, hw, hw, LANES, LANES, 2 * hw))
    yc = _dense_attn(qc, kc, vc, nq=1, tq=1024, tk=DENSE_TK, n_q_tok=SEQ, diff=True,
                     lamv=c_lambda[0].astype(F32), subln=c_subln[0].astype(F32).reshape(1, LANES),
                     lam0=_lambda_init(1))
    yd = _win_attn(qd, kd, vd, d_sinks[0].astype(F32) * LOG2E)
    return _out_proj(x1, None, yc, yd, zz1, w_out1, mod1, final_norm.astype(F32).reshape(1, D_MODEL))
```

```python
import functools
import math

import numpy as np
import jax
import jax.numpy as jnp
from jax import lax
from jax.experimental import pallas as pl
from jax.experimental.pallas import tpu as pltpu

D_MODEL = 1024
BATCH = 4
SEQ = 8192
DEPTH = 2
GRID_W = 64
CTX_LEN = 256
HEAD_DIM = 64
ROPE_THETA = 10000.0
EPS = 1e-6
NEG_INF = -1e30
A_HEADS = 8
B_HEADS = 8
NA_ROWS = 8
NA_COLS = 16
C_HEADS = 4
D_HEADS = 8
D_WINDOW = 128
SCALE = HEAD_DIM ** -0.5
LOG2E = math.log2(math.e)
QSCALE = SCALE * LOG2E

LANES = 128
HALF = HEAD_DIM
TOK = SEQ + CTX_LEN
TM = 256
N_LAT_TM = SEQ // TM
TM_OUT = 512
VT = 128
QB = 256
DENSE_TK = 1024
TQ_BLK = 128
N_LAT_TQ = SEQ // TQ_BLK
N_CTX_TQ = CTX_LEN // TQ_BLK
GRID_ROWS = SEQ // GRID_W
NB_ROWS = 10
NB_KEYS = NB_ROWS * GRID_W
NB_BLKS = NB_KEYS // TQ_BLK
WIN_BLKS = 1 + 2 * (D_WINDOW // TQ_BLK)
VMEM_LIMIT = 56 * 1024 * 1024

HEAD_PERM = (0, 4, 1, 5, 2, 6, 3, 7)
F32 = jnp.float32
BF16 = jnp.bfloat16


def _lambda_init(layer):
    return 0.8 - 0.6 * math.exp(-0.3 * layer)


def _lane_iota(rows):
    return lax.broadcasted_iota(jnp.int32, (rows, LANES), 1)


def _silu(z):
    return z / (1.0 + jnp.exp(-z))


def _mod_kernel(c_ref, w_ref, b_ref, o_ref):
    o_ref[0] = jnp.dot(_silu(c_ref[...]), w_ref[0], preferred_element_type=F32) + b_ref[0]


def _modulation(cs, w_mod, b_mod):
    tn = D_MODEL
    return pl.pallas_call(
        _mod_kernel,
        out_shape=jax.ShapeDtypeStruct((DEPTH, 8, 3 * D_MODEL), F32),
        grid=(DEPTH, 3 * D_MODEL // tn),
        in_specs=[
            pl.BlockSpec((8, D_MODEL), lambda l, j: (0, 0)),
            pl.BlockSpec((1, D_MODEL, tn), lambda l, j: (l, 0, j)),
            pl.BlockSpec((1, 1, tn), lambda l, j: (l, 0, j)),
        ],
        out_specs=pl.BlockSpec((1, 8, tn), lambda l, j: (l, 0, j)),
        compiler_params=pltpu.CompilerParams(
            dimension_semantics=("arbitrary", "arbitrary"), vmem_limit_bytes=VMEM_LIMIT),
        name="modulation",
    )(cs, w_mod, b_mod.reshape(DEPTH, 1, 3 * D_MODEL))


def _swap16(x):
    lane = _lane_iota(x.shape[0])
    fwd = pltpu.roll(x, LANES - 16, 1)
    bwd = pltpu.roll(x, 16, 1)
    return jnp.where((lane & 16) == 0, fwd, bwd)


def _proj_kernel(*refs, plan):
    x_ref, xc_ref, mod_ref, w_ref, qg_ref, kg_ref, cos_ref, sin_ref, seg_ref = refs[:9]
    out_refs = refs[9:]

    xt = jnp.where(pl.program_id(1) < N_LAT_TM, x_ref[0], xc_ref[0])
    ms = jnp.mean(xt * xt, axis=-1, keepdims=True)
    xn = xt * lax.rsqrt(ms + EPS)
    shift = mod_ref[0, :, 0:D_MODEL]
    scale = mod_ref[0, :, D_MODEL:2 * D_MODEL]
    h = (xn * (1.0 + scale) + shift).astype(BF16)

    cos = cos_ref[...]
    sin = sin_ref[...]
    seg = seg_ref[...]
    col = 0
    for width, outs in plan:
        acc = jnp.dot(h, w_ref[:, col:col + width], preferred_element_type=F32)
        col += width
        for out_idx, lo, hi, norm, rope, scale_q, transpose in outs:
            pieces = []
            for c0 in range(lo, hi, LANES):
                blk = acc[:, c0:c0 + LANES]
                if norm:
                    sq = blk * blk
                    sq_hi = sq.astype(BF16)
                    sq_lo = (sq - sq_hi.astype(F32)).astype(BF16)
                    ssum = (jnp.dot(sq_hi, seg, preferred_element_type=F32)
                            + jnp.dot(sq_lo, seg, preferred_element_type=F32))
                    gain = qg_ref[...] if norm == "q" else kg_ref[...]
                    blk = blk * lax.rsqrt(ssum * (1.0 / HEAD_DIM) + EPS) * gain
                if rope:
                    blk = blk * cos + _swap16(blk) * sin
                if scale_q:
                    blk = blk * QSCALE
                if transpose:
                    blk_t = blk.T.astype(BF16)
                    for u in range(TM // VT):
                        out_refs[out_idx][0, (c0 - lo) // LANES, u] = blk_t[:, u * VT:(u + 1) * VT]
                    continue
                pieces.append(blk.astype(BF16))
            if pieces:
                out_refs[out_idx][0] = pieces[0] if len(pieces) == 1 else jnp.concatenate(pieces, axis=-1)


def _in_proj(x, xc, mod, w, qg, kg, cos, sin, seg, plan, out_widths):
    n_tiles = TOK // TM
    n_in = w.shape[1]
    in_specs = [
        pl.BlockSpec((1, TM, D_MODEL), lambda b, i: (b, jnp.minimum(i, N_LAT_TM - 1), 0)),
        pl.BlockSpec((1, CTX_LEN, D_MODEL), lambda b, i: (b, 0, 0)),
        pl.BlockSpec((1, 1, 3 * D_MODEL), lambda b, i: (jnp.where(i < N_LAT_TM, b, BATCH), 0, 0)),
        pl.BlockSpec((D_MODEL, n_in), lambda b, i: (0, 0)),
        pl.BlockSpec((1, LANES), lambda b, i: (0, 0)),
        pl.BlockSpec((1, LANES), lambda b, i: (0, 0)),
        pl.BlockSpec((TM, LANES), lambda b, i: (i, 0)),
        pl.BlockSpec((TM, LANES), lambda b, i: (i, 0)),
        pl.BlockSpec((LANES, LANES), lambda b, i: (0, 0)),
    ]
    transposed = {o[0] for _, outs in plan for o in outs if o[6]}
    out_shape, out_specs = [], []
    for idx, wd in enumerate(out_widths):
        if idx in transposed:
            out_shape.append(jax.ShapeDtypeStruct((BATCH, wd // LANES, TOK // VT, LANES, VT), BF16))
            out_specs.append(pl.BlockSpec((1, wd // LANES, TM // VT, LANES, VT),
                                          lambda b, i: (b, 0, i, 0, 0)))
        else:
            out_shape.append(jax.ShapeDtypeStruct((BATCH, TOK, wd), BF16))
            out_specs.append(pl.BlockSpec((1, TM, wd), lambda b, i: (b, i, 0)))
    return pl.pallas_call(
        functools.partial(_proj_kernel, plan=plan),
        out_shape=tuple(out_shape),
        grid=(BATCH, n_tiles),
        in_specs=in_specs,
        out_specs=tuple(out_specs),
        compiler_params=pltpu.CompilerParams(
            dimension_semantics=("arbitrary", "arbitrary"), vmem_limit_bytes=VMEM_LIMIT),
        name="in_proj",
    )(x, xc, mod, w, qg, kg, cos, sin, seg)


def _out_kernel(*refs, final):
    x_ref, y1_ref, y2_ref, z_ref, w_ref, mod_ref = refs[:6]
    if final:
        fn_ref, o_ref = refs[6:]
    else:
        (o_ref,) = refs[6:]
    y = jnp.concatenate([y1_ref[0], y2_ref[0]], axis=-1).astype(F32)
    g = (y * _silu(z_ref[0].astype(F32))).astype(BF16)
    d = jnp.dot(g, w_ref[...], preferred_element_type=F32)
    gate = mod_ref[0, :, 2 * D_MODEL:3 * D_MODEL]
    xo = x_ref[0] + gate * d
    if final:
        ms = jnp.mean(xo * xo, axis=-1, keepdims=True)
        xo = xo * lax.rsqrt(ms + EPS) * fn_ref[...]
    o_ref[0] = xo


def _out_proj(x, y1, y2, z, w, mod, final_gain, *, context=False):
    final = final_gain is not None
    n_tok = CTX_LEN if context else SEQ
    tm = TM if context else TM_OUT
    row0 = SEQ // tm if context else 0
    half = y1.shape[-1]

    def mod_map(b, i):
        return (BATCH if context else b), 0, 0

    in_specs = [
        pl.BlockSpec((1, tm, D_MODEL), lambda b, i: (b, i, 0)),
        pl.BlockSpec((1, tm, half), lambda b, i: (b, row0 + i, 0)),
        pl.BlockSpec((1, tm, half), lambda b, i: (b, row0 + i, 0)),
        pl.BlockSpec((1, tm, 2 * half), lambda b, i: (b, row0 + i, 0)),
        pl.BlockSpec((2 * half, D_MODEL), lambda b, i: (0, 0)),
        pl.BlockSpec((1, 1, 3 * D_MODEL), mod_map),
    ]
    args = (x, y1, y2, z, w, mod)
    if final:
        in_specs.append(pl.BlockSpec((1, D_MODEL), lambda b, i: (0, 0)))
        args = args + (final_gain,)
    return pl.pallas_call(
        functools.partial(_out_kernel, final=final),
        out_shape=jax.ShapeDtypeStruct((BATCH, n_tok, D_MODEL), F32),
        grid=(BATCH, n_tok // tm),
        in_specs=in_specs,
        out_specs=pl.BlockSpec((1, tm, D_MODEL), lambda b, i: (b, i, 0)),
        compiler_params=pltpu.CompilerParams(
            dimension_semantics=("arbitrary", "arbitrary"), vmem_limit_bytes=VMEM_LIMIT),
        name="out_proj",
    )(*args)


def _write_qt(q_ref, qt_scr, nq, tq):
    feat = lax.broadcasted_iota(jnp.int32, (LANES, VT), 0)
    for j in range(nq):
        for u in range(tq // VT):
            qt = q_ref[0, j, u].astype(F32)
            lo = (2 * j) * tq + u * VT
            hi = (2 * j + 1) * tq + u * VT
            qt_scr[:, lo:lo + VT] = jnp.where(feat < HALF, qt, 0.0).astype(BF16)
            qt_scr[:, hi:hi + VT] = jnp.where(feat >= HALF, qt, 0.0).astype(BF16)


def _qt_spec(nq, tq, index_map):
    return pl.BlockSpec((1, nq, tq // VT, LANES, VT), index_map)


def _merge_qt(ot, nq, tq):
    feat = lax.broadcasted_iota(jnp.int32, (LANES, tq), 0)
    pieces = []
    for j in range(nq):
        blk = jnp.where(feat < HALF, ot[:, (2 * j) * tq:(2 * j + 1) * tq],
                        ot[:, (2 * j + 1) * tq:(2 * j + 2) * tq])
        pieces.append(blk.T.astype(BF16))
    return pieces[0] if nq == 1 else jnp.concatenate(pieces, axis=-1)


def _dense_kernel(*refs, nq, tq, tk, n_lat_tiles, diff, lam0):
    if diff:
        q_ref, k_ref, vt_ref, lamv_ref, subln_ref, o_ref, qt_scr, m_scr, l_scr, acc_scr, s_scr = refs
    else:
        q_ref, k_ref, vt_ref, o_ref, qt_scr, m_scr, l_scr, acc_scr, s_scr = refs
    n_q = 2 * nq * tq
    _write_qt(q_ref, qt_scr, nq, tq)
    m_scr[...] = jnp.full((1, n_q), -jnp.inf, F32)
    l_scr[...] = jnp.zeros((1, n_q), F32)
    acc_scr[...] = jnp.zeros((LANES, n_q), F32)

    n_blk = n_q // QB
    ctx_chunk = (SEQ, SEQ // VT, CTX_LEN // VT)
    lat_chunks = [(c * tk, c * (tk // VT), tk // VT) for c in range(SEQ // tk)]

    def scores(chunk, n, half):
        start, _, subs = chunk
        kc = k_ref[0, pl.ds(start, subs * VT), :]
        s_scr[half, n, 0:subs * VT, :] = jnp.dot(kc, qt_scr[:, n * QB:(n + 1) * QB],
                                                 preferred_element_type=F32)

    def softmax_pv(chunk, n, half):
        _, tile0, subs = chunk
        cols = slice(n * QB, (n + 1) * QB)
        vtc = jnp.concatenate([vt_ref[0, 0, tile0 + u] for u in range(subs)], axis=1)
        s = s_scr[half, n, 0:subs * VT, :]
        m_prev = m_scr[:, cols]
        m_new = jnp.maximum(m_prev, jnp.max(s, axis=0, keepdims=True))
        alpha = jnp.exp2(m_prev - m_new)
        p = jnp.exp2(s - m_new)
        l_scr[:, cols] = alpha * l_scr[:, cols] + jnp.sum(p, axis=0, keepdims=True)
        acc_scr[:, cols] = alpha * acc_scr[:, cols] + jnp.dot(
            vtc, p.astype(BF16), preferred_element_type=F32)
        m_scr[:, cols] = m_new

    def half_step(cur, half, nxt):
        for n in range(n_blk):
            if nxt is not None:
                scores(nxt, n, 1 - half)
            softmax_pv(cur, n, half)

    def latent_keys():
        half_step(ctx_chunk, 0, lat_chunks[0])
        for c, chunk in enumerate(lat_chunks):
            half_step(chunk, (c + 1) % 2, lat_chunks[c + 1] if c + 1 < len(lat_chunks) else None)

    def context_keys_only():
        half_step(ctx_chunk, 0, None)

    for n in range(n_blk):
        scores(ctx_chunk, n, 0)
    is_latent = pl.program_id(2) < n_lat_tiles
    pl.when(is_latent)(latent_keys)
    pl.when(jnp.logical_not(is_latent))(context_keys_only)

    ot = acc_scr[...] * (1.0 / l_scr[...])
    if diff:
        lv = lamv_ref[...]
        lam = (jnp.exp(jnp.sum(lv[0:1] * lv[1:2], axis=-1, keepdims=True))
               - jnp.exp(jnp.sum(lv[2:3] * lv[3:4], axis=-1, keepdims=True)) + lam0)
        dlt = (ot[:, 0:tq] - lam * ot[:, tq:2 * tq]).T
        ms = jnp.mean(dlt * dlt, axis=-1, keepdims=True)
        y = dlt * lax.rsqrt(ms + EPS) * subln_ref[...] * (1.0 - lam0)
        o_ref[0] = y.astype(BF16)
    else:
        o_ref[0] = _merge_qt(ot, nq, tq)


def _dense_attn(q, k, vt, *, nq, tq, tk, n_q_tok, diff=False, lamv=None, subln=None, lam0=0.0):
    groups = q.shape[1] // nq
    n_tiles = n_q_tok // tq
    n_lat_tiles = SEQ // tq
    n_q = 2 * nq * tq
    in_specs = [
        _qt_spec(nq, tq, lambda b, g, i: (b, g, i, 0, 0)),
        pl.BlockSpec((1, TOK, LANES), lambda b, g, i: (b, 0, g)),
        pl.BlockSpec((1, 1, TOK // VT, LANES, VT), lambda b, g, i: (b, g, 0, 0, 0)),
    ]
    args = (q, k, vt)
    if diff:
        in_specs += [pl.BlockSpec((4, HEAD_DIM), lambda b, g, i: (0, 0)),
                     pl.BlockSpec((1, LANES), lambda b, g, i: (0, 0))]
        args = args + (lamv, subln)
    return pl.pallas_call(
        functools.partial(_dense_kernel, nq=nq, tq=tq, tk=tk, n_lat_tiles=n_lat_tiles,
                          diff=diff, lam0=lam0),
        out_shape=jax.ShapeDtypeStruct((BATCH, n_q_tok, groups * nq * LANES), BF16),
        grid=(BATCH, groups, n_tiles),
        in_specs=in_specs,
        out_specs=pl.BlockSpec((1, tq, nq * LANES), lambda b, g, i: (b, i, g)),
        scratch_shapes=[pltpu.VMEM((LANES, n_q), BF16),
                        pltpu.VMEM((1, n_q), F32),
                        pltpu.VMEM((1, n_q), F32),
                        pltpu.VMEM((LANES, n_q), F32),
                        pltpu.VMEM((2, n_q // QB, tk, QB), F32)],
        compiler_params=pltpu.CompilerParams(
            dimension_semantics=("arbitrary", "arbitrary", "arbitrary"),
            vmem_limit_bytes=VMEM_LIMIT),
        name="dense_diff_attn" if diff else "dense_gqa_attn",
    )(*args)


def _nb_kernel(q_ref, k_ref, vt_ref, bias_ref, o_ref, qt_scr, s_scr):
    tq = TQ_BLK
    nq = B_HEADS // 2
    n_keys = NB_KEYS + CTX_LEN
    first = jnp.clip(pl.program_id(1) - 2, 0, N_LAT_TQ - NB_BLKS)
    start = pl.multiple_of(first * tq, tq)
    _write_qt(q_ref, qt_scr, nq, tq)
    for j in range(nq):
        cs = slice(j * LANES, (j + 1) * LANES)
        kcat = jnp.concatenate([k_ref[0, pl.ds(start, NB_KEYS), cs], k_ref[0, SEQ:TOK, cs]], axis=0)
        s_scr[j] = jnp.dot(kcat, qt_scr[:, (2 * j) * tq:(2 * j + 2) * tq], preferred_element_type=F32)
    outs = []
    for j in range(nq):
        vtcat = jnp.concatenate([vt_ref[0, j, first + u] for u in range(NB_BLKS)]
                                + [vt_ref[0, j, SEQ // VT + u] for u in range(CTX_LEN // VT)],
                                axis=1)
        s_n = s_scr[j, 0:NB_KEYS, :] + bias_ref[0, j]
        s_c = s_scr[j, NB_KEYS:n_keys, :]
        m = jnp.maximum(jnp.max(s_n, axis=0, keepdims=True), jnp.max(s_c, axis=0, keepdims=True))
        p = jnp.concatenate([jnp.exp2(s_n - m), jnp.exp2(s_c - m)], axis=0)
        l = jnp.sum(p, axis=0, keepdims=True)
        outs.append(jnp.dot(vtcat, p.astype(BF16), preferred_element_type=F32) * (1.0 / l))
    o_ref[0] = _merge_qt(jnp.concatenate(outs, axis=1), nq, tq)


def _nb_case(i):
    last = N_LAT_TQ - 1
    return jnp.where(i == 0, 0, jnp.where(i == 1, 1, jnp.where(
        i == last - 1, 3, jnp.where(i == last, 4, jnp.where(i > last, 5, 2)))))


def _nb_attn(q, k, vt, bias):
    width = B_HEADS * HEAD_DIM
    n_pairs = B_HEADS // 2
    n_tiles = N_LAT_TQ + N_CTX_TQ

    in_specs = [_qt_spec(n_pairs, TQ_BLK, lambda b, i: (b, 0, i, 0, 0)),
                pl.BlockSpec((1, TOK, width), lambda b, i: (b, 0, 0), pipeline_mode=pl.Buffered(1)),
                pl.BlockSpec((1, n_pairs, TOK // VT, LANES, VT), lambda b, i: (b, 0, 0, 0, 0),
                             pipeline_mode=pl.Buffered(1)),
                pl.BlockSpec((1, n_pairs, NB_KEYS, 2 * TQ_BLK), lambda b, i: (_nb_case(i), 0, 0, 0))]
    return pl.pallas_call(
        _nb_kernel,
        out_shape=jax.ShapeDtypeStruct((BATCH, TOK, width), BF16),
        grid=(BATCH, n_tiles),
        in_specs=in_specs,
        out_specs=pl.BlockSpec((1, TQ_BLK, width), lambda b, i: (b, i, 0)),
        scratch_shapes=[pltpu.VMEM((LANES, B_HEADS * TQ_BLK), BF16),
                        pltpu.VMEM((n_pairs, NB_KEYS + CTX_LEN, 2 * TQ_BLK), F32)],
        compiler_params=pltpu.CompilerParams(
            dimension_semantics=("arbitrary", "arbitrary"), vmem_limit_bytes=VMEM_LIMIT),
        name="neighbourhood_attn",
    )(q, k, vt, bias)


def _nb_bias_table(rpb):
    grid_spec = pltpu.PrefetchScalarGridSpec(
        num_scalar_prefetch=1,
        grid=(6, B_HEADS),
        in_specs=[],
        out_specs=pl.BlockSpec((1, 1, NB_KEYS, TQ_BLK), lambda c, h, r: (c, h // 2, 0, h % 2)),
    )
    return pl.pallas_call(
        _nb_bias_kernel,
        out_shape=jax.ShapeDtypeStruct((6, B_HEADS // 2, NB_KEYS, 2 * TQ_BLK), F32),
        grid_spec=grid_spec,
        compiler_params=pltpu.CompilerParams(dimension_semantics=("arbitrary", "arbitrary")),
        name="nb_bias_table",
    )(rpb.astype(F32).reshape(-1))


def _nb_bias_kernel(rpb_ref, o_ref):
    c = pl.program_id(0)
    h = pl.program_id(1)
    last = N_LAT_TQ - 1
    tile = jnp.where(c == 3, last - 1, jnp.where(c == 4, last, jnp.minimum(c, 2)))
    r0 = 2 * tile
    base = jnp.clip(2 * tile - 4, 0, GRID_ROWS - NB_ROWS)
    n_dr = 2 * NA_ROWS - 1
    n_dc = 2 * NA_COLS - 1
    kc = lax.broadcasted_iota(jnp.int32, (GRID_W, LANES), 0)
    lane = lax.broadcasted_iota(jnp.int32, (GRID_W, LANES), 1)
    qc = lane & (GRID_W - 1)
    second = lane >= GRID_W
    dc = kc - qc + (NA_COLS - 1)
    cs = jnp.clip(qc - NA_COLS // 2, 0, GRID_W - NA_COLS)
    col_ok = jnp.where(kc >= cs, jnp.where(kc < cs + NA_COLS, 1, 0), 0)
    rs0 = jnp.clip(r0 - NA_ROWS // 2, 0, GRID_ROWS - NA_ROWS)
    rs1 = jnp.clip(r0 + 1 - NA_ROWS // 2, 0, GRID_ROWS - NA_ROWS)
    for krow in range(NB_ROWS):
        kr = base + krow
        dr = kr - r0 + (NA_ROWS - 1)
        ok0 = jnp.where((kr >= rs0) & (kr < rs0 + NA_ROWS) & (c < 5), 1, 0)
        ok1 = jnp.where((kr >= rs1) & (kr < rs1 + NA_ROWS) & (c < 5), 1, 0)
        off0 = (h * n_dr + jnp.clip(dr, 0, n_dr - 1)) * n_dc
        off1 = (h * n_dr + jnp.clip(dr - 1, 0, n_dr - 1)) * n_dc
        acc = jnp.zeros((GRID_W, LANES), F32)
        for j in range(n_dc):
            val = jnp.where(second, rpb_ref[off1 + j], rpb_ref[off0 + j])
            acc = jnp.where(dc == j, val, acc)
        ok = col_ok * jnp.where(second, ok1, ok0)
        o_ref[0, 0, krow * GRID_W:(krow + 1) * GRID_W, :] = jnp.where(ok > 0, acc * LOG2E, NEG_INF)


def _win_kernel(sink_ref, q_ref, k_ref, vt_ref, o_ref, qt_scr, s_scr):
    tq = TQ_BLK
    nq = D_HEADS // 2
    n_blk = 2 * nq * tq // QB
    i = pl.program_id(1)
    first = jnp.clip(i - D_WINDOW // tq, 0, N_LAT_TQ - WIN_BLKS)
    start = pl.multiple_of(first * tq, tq)
    _write_qt(q_ref, qt_scr, nq, tq)
    kcat = jnp.concatenate([k_ref[0, pl.ds(start, WIN_BLKS * tq), :], k_ref[0, SEQ:TOK, :]],
                           axis=0)
    vtcat = jnp.concatenate([vt_ref[0, 0, first + u] for u in range(WIN_BLKS)]
                            + [vt_ref[0, 0, SEQ // VT + u] for u in range(CTX_LEN // VT)],
                            axis=1)
    for n in range(n_blk):
        s_scr[n] = jnp.dot(kcat, qt_scr[:, n * QB:(n + 1) * QB], preferred_element_type=F32)

    span = WIN_BLKS * tq
    n_keys = span + CTX_LEN
    kk = lax.broadcasted_iota(jnp.int32, (n_keys, QB), 0)
    qq = lax.broadcasted_iota(jnp.int32, (n_keys, QB), 1) & (tq - 1)
    rel = kk - qq + (first - i) * tq
    mask = jnp.where(kk >= span, 0.0, jnp.where(jnp.abs(rel) <= D_WINDOW, 0.0, NEG_INF))

    outs = []
    for n in range(n_blk):
        sinks = jnp.concatenate(
            [jnp.full((1, tq), sink_ref[HEAD_PERM[n * QB // tq + g]], F32) for g in range(QB // tq)], axis=1)
        s = s_scr[n] + mask
        m = jnp.maximum(jnp.max(s, axis=0, keepdims=True), sinks)
        p = jnp.exp2(s - m)
        l = jnp.sum(p, axis=0, keepdims=True) + jnp.exp2(sinks - m)
        outs.append(jnp.dot(vtcat, p.astype(BF16), preferred_element_type=F32) * (1.0 / l))
    o_ref[0] = _merge_qt(jnp.concatenate(outs, axis=1), nq, tq)


def _win_attn(q, k, vt, sinks):
    width = D_HEADS * HEAD_DIM
    n_q = D_HEADS * TQ_BLK
    grid_spec = pltpu.PrefetchScalarGridSpec(
        num_scalar_prefetch=1,
        grid=(BATCH, N_LAT_TQ),
        in_specs=[_qt_spec(D_HEADS // 2, TQ_BLK, lambda b, i, s: (b, 0, i, 0, 0)),
                  pl.BlockSpec((1, TOK, LANES), lambda b, i, s: (b, 0, 0)),
                  pl.BlockSpec((1, 1, TOK // VT, LANES, VT), lambda b, i, s: (b, 0, 0, 0, 0))],
        out_specs=pl.BlockSpec((1, TQ_BLK, width), lambda b, i, s: (b, i, 0)),
        scratch_shapes=[pltpu.VMEM((LANES, n_q), BF16),
                        pltpu.VMEM((n_q // QB, WIN_BLKS * TQ_BLK + CTX_LEN, QB), F32)],
    )
    return pl.pallas_call(
        _win_kernel,
        out_shape=jax.ShapeDtypeStruct((BATCH, SEQ, width), BF16),
        grid_spec=grid_spec,
        compiler_params=pltpu.CompilerParams(
            dimension_semantics=("arbitrary", "arbitrary"), vmem_limit_bytes=VMEM_LIMIT),
        name="window_attn",
    )(sinks, q, k, vt)


def _rope_tables():
    t = jnp.arange(SEQ, dtype=jnp.int32)
    row = (t // GRID_W).astype(F32)
    col = (t % GRID_W).astype(F32)
    quarter = HEAD_DIM // 4
    inv_freq = ROPE_THETA ** (-jnp.arange(quarter, dtype=F32) / quarter)
    ang_r = row[:, None] * inv_freq
    ang_c = col[:, None] * inv_freq
    cr, sr, cc, sc = jnp.cos(ang_r), jnp.sin(ang_r), jnp.cos(ang_c), jnp.sin(ang_c)
    cos = jnp.tile(jnp.concatenate([cr, cr, cc, cc], axis=-1), (1, LANES // HEAD_DIM))
    sin = jnp.tile(jnp.concatenate([-sr, sr, -sc, sc], axis=-1), (1, LANES // HEAD_DIM))
    cos = jnp.concatenate([cos, jnp.ones((CTX_LEN, LANES), F32)], axis=0)
    sin = jnp.concatenate([sin, jnp.zeros((CTX_LEN, LANES), F32)], axis=0)
    return cos, sin


def _head_cols(perm):
    return np.concatenate([np.arange(HEAD_DIM) + HEAD_DIM * h for h in perm])


def _take_static(w, idx, axis):
    breaks = np.flatnonzero(np.diff(idx) != 1) + 1
    runs = np.split(idx, breaks)
    parts = [lax.slice_in_dim(w, int(r[0]), int(r[-1]) + 1, axis=axis) for r in runs]
    return parts[0] if len(parts) == 1 else jnp.concatenate(parts, axis=axis)


def kernel(x, c, ctx, c_ctx, w_mod, b_mod, w_in_even, w_out_even, a_q_norm, a_k_norm, b_rpb,
           w_in_odd, w_out_odd, c_lambda, c_subln, d_sinks, final_norm):
    assert DEPTH == 2 and x.shape == (BATCH, SEQ, D_MODEL) and ctx.shape == (BATCH, CTX_LEN, D_MODEL)
    perm = _head_cols(HEAD_PERM)
    hw = A_HEADS * HEAD_DIM

    cs = jnp.concatenate([c, c_ctx[None], jnp.zeros((8 - BATCH - 1, D_MODEL), F32)], axis=0)
    mod = _modulation(cs, w_mod, b_mod)
    mod0 = mod[0].reshape(8, 1, 3 * D_MODEL)
    mod1 = mod[1].reshape(8, 1, 3 * D_MODEL)

    cos, sin = _rope_tables()
    seg = jnp.asarray(np.kron(np.eye(LANES // HEAD_DIM), np.ones((HEAD_DIM, HEAD_DIM))), BF16)
    tile2 = LANES // HEAD_DIM
    qg = jnp.tile(a_q_norm[0].astype(F32), tile2).reshape(1, LANES)
    kg = jnp.tile(a_k_norm[0].astype(F32), tile2).reshape(1, LANES)

    z0 = 2304
    cols0 = np.concatenate([perm, np.arange(hw, z0), z0 + perm, np.arange(z0 + hw, z0 + 2 * hw)])
    w_in0 = _take_static(w_in_even[0], cols0, 1).astype(BF16)
    w_out0 = _take_static(w_out_even[0], np.concatenate([perm, np.arange(hw, 2 * hw)]), 0).astype(BF16)
    plan0 = (
        (hw, ((0, 0, hw, "q", True, True, True),)),
        (2 * LANES, ((1, 0, LANES, "k", True, False, False),
                     (2, LANES, 2 * LANES, None, False, False, True))),
        (hw, ((3, 0, hw, None, False, True, True),)),
        (hw, ((4, 0, hw, None, False, False, False),)),
        (hw, ((5, 0, hw, None, False, False, True),)),
        (2 * hw, ((6, 0, 2 * hw, None, False, False, False),)),
    )
    qa, ka, va, qb, kb, vb, zz = _in_proj(x, ctx, mod0, w_in0, qg, kg, cos, sin, seg, plan0,
                                          (hw, LANES, LANES, hw, hw, hw, 2 * hw))
    ya = _dense_attn(qa, ka, va, nq=4, tq=256, tk=DENSE_TK, n_q_tok=TOK)
    yb = _nb_attn(qb, kb, vb, _nb_bias_table(b_rpb[0]))
    x1 = _out_proj(x, ya, yb, zz, w_out0, mod0, None)
    xc1 = _out_proj(ctx, ya, yb, zz, w_out0, mod0, None, context=True)

    cols1 = np.concatenate([np.arange(0, 3 * hw), 3 * hw + perm, np.arange(4 * hw, z0 + hw), z0 + hw + perm])
    w_in1 = _take_static(w_in_odd[0], cols1, 1).astype(BF16)
    w_out1 = _take_static(w_out_odd[0], np.concatenate([np.arange(hw), hw + perm]), 0).astype(BF16)
    plan1 = (
        (hw, ((0, 0, hw, None, True, True, True),)),
        (hw, ((1, 0, hw, None, True, False, False),)),
        (hw, ((2, 0, hw, None, False, False, True),)),
        (hw, ((3, 0, hw, None, True, True, True),)),
        (2 * LANES, ((4, 0, LANES, None, True, False, False),
                     (5, LANES, 2 * LANES, None, False, False, True))),
        (2 * hw, ((6, 0, 2 * hw, None, False, False, False),)),
    )
    qc, kc, vc, qd, kd, vd, zz1 = _in_proj(x1, xc1, mod1, w_in1, qg, kg, cos, sin, seg, plan1,
                                           (hw, hw, hw, hw, LANES, LANES, 2 * hw))
    yc = _dense_attn(qc, kc, vc, nq=1, tq=1024, tk=DENSE_TK, n_q_tok=SEQ, diff=True,
                     lamv=c_lambda[0].astype(F32), subln=c_subln[0].astype(F32).reshape(1, LANES),
                     lam0=_lambda_init(1))
    yd = _win_attn(qd, kd, vd, d_sinks[0].astype(F32) * LOG2E)
    return _out_proj(x1, yc, yd, zz1, w_out1, mod1, final_norm.astype(F32).reshape(1, D_MODEL))
```

```python
import functools
import math

import numpy as np
import jax
import jax.numpy as jnp
from jax import lax
from jax.experimental import pallas as pl
from jax.experimental.pallas import tpu as pltpu

D_MODEL = 1024
BATCH = 4
SEQ = 8192
DEPTH = 2
GRID_W = 64
CTX_LEN = 256
HEAD_DIM = 64
ROPE_THETA = 10000.0
EPS = 1e-6
NEG_INF = -1e30
A_HEADS = 8
B_HEADS = 8
NA_ROWS = 8
NA_COLS = 16
C_HEADS = 4
D_HEADS = 8
D_WINDOW = 128
SCALE = HEAD_DIM ** -0.5
LOG2E = math.log2(math.e)
QSCALE = SCALE * LOG2E

LANES = 128
HALF = HEAD_DIM
TOK = SEQ + CTX_LEN
TM = 256
N_LAT_TM = SEQ // TM
TM_OUT = 1024
VT = 128
QB = 256
DENSE_TK = 1024
TQ_BLK = 128
N_LAT_TQ = SEQ // TQ_BLK
N_CTX_TQ = CTX_LEN // TQ_BLK
GRID_ROWS = SEQ // GRID_W
NB_ROWS = 10
NB_KEYS = NB_ROWS * GRID_W
NB_BLKS = NB_KEYS // TQ_BLK
WIN_BLKS = 1 + 2 * (D_WINDOW // TQ_BLK)
VMEM_LIMIT = 56 * 1024 * 1024

HEAD_PERM = (0, 4, 1, 5, 2, 6, 3, 7)
F32 = jnp.float32
BF16 = jnp.bfloat16


def _lambda_init(layer):
    return 0.8 - 0.6 * math.exp(-0.3 * layer)


def _lane_iota(rows):
    return lax.broadcasted_iota(jnp.int32, (rows, LANES), 1)


def _silu(z):
    return z / (1.0 + jnp.exp(-z))


def _mod_kernel(c_ref, w_ref, b_ref, o_ref):
    o_ref[0] = jnp.dot(_silu(c_ref[...]), w_ref[0], preferred_element_type=F32) + b_ref[0]


def _modulation(cs, w_mod, b_mod):
    tn = D_MODEL
    return pl.pallas_call(
        _mod_kernel,
        out_shape=jax.ShapeDtypeStruct((DEPTH, 8, 3 * D_MODEL), F32),
        grid=(DEPTH, 3 * D_MODEL // tn),
        in_specs=[
            pl.BlockSpec((8, D_MODEL), lambda l, j: (0, 0)),
            pl.BlockSpec((1, D_MODEL, tn), lambda l, j: (l, 0, j)),
            pl.BlockSpec((1, 1, tn), lambda l, j: (l, 0, j)),
        ],
        out_specs=pl.BlockSpec((1, 8, tn), lambda l, j: (l, 0, j)),
        compiler_params=pltpu.CompilerParams(
            dimension_semantics=("arbitrary", "arbitrary"), vmem_limit_bytes=VMEM_LIMIT),
        name="modulation",
    )(cs, w_mod, b_mod.reshape(DEPTH, 1, 3 * D_MODEL))


def _swap16(x):
    lane = _lane_iota(x.shape[0])
    fwd = pltpu.roll(x, LANES - 16, 1)
    bwd = pltpu.roll(x, 16, 1)
    return jnp.where((lane & 16) == 0, fwd, bwd)


def _proj_kernel(*refs, plan):
    x_ref, xc_ref, mod_ref, w_ref, qg_ref, kg_ref, cos_ref, sin_ref, seg_ref = refs[:9]
    out_refs = refs[9:]

    xt = jnp.where(pl.program_id(1) < N_LAT_TM, x_ref[0], xc_ref[0])
    ms = jnp.mean(xt * xt, axis=-1, keepdims=True)
    xn = xt * lax.rsqrt(ms + EPS)
    shift = mod_ref[0, :, 0:D_MODEL]
    scale = mod_ref[0, :, D_MODEL:2 * D_MODEL]
    h = (xn * (1.0 + scale) + shift).astype(BF16)

    cos = cos_ref[...]
    sin = sin_ref[...]
    seg = seg_ref[...]
    col = 0
    for width, outs in plan:
        acc = jnp.dot(h, w_ref[:, col:col + width], preferred_element_type=F32)
        col += width
        for out_idx, lo, hi, norm, rope, scale_q, transpose in outs:
            pieces = []
            for c0 in range(lo, hi, LANES):
                blk = acc[:, c0:c0 + LANES]
                if norm:
                    sq = blk * blk
                    sq_hi = sq.astype(BF16)
                    sq_lo = (sq - sq_hi.astype(F32)).astype(BF16)
                    ssum = (jnp.dot(sq_hi, seg, preferred_element_type=F32)
                            + jnp.dot(sq_lo, seg, preferred_element_type=F32))
                    gain = qg_ref[...] if norm == "q" else kg_ref[...]
                    blk = blk * lax.rsqrt(ssum * (1.0 / HEAD_DIM) + EPS) * gain
                if rope:
                    blk = blk * cos + _swap16(blk) * sin
                if scale_q:
                    blk = blk * QSCALE
                if transpose:
                    blk_t = blk.T.astype(BF16)
                    for u in range(TM // VT):
                        out_refs[out_idx][0, (c0 - lo) // LANES, u] = blk_t[:, u * VT:(u + 1) * VT]
                    continue
                pieces.append(blk.astype(BF16))
            if pieces:
                out_refs[out_idx][0] = pieces[0] if len(pieces) == 1 else jnp.concatenate(pieces, axis=-1)


def _in_proj(x, xc, mod, w, qg, kg, cos, sin, seg, plan, out_widths):
    n_tiles = TOK // TM
    n_in = w.shape[1]
    in_specs = [
        pl.BlockSpec((1, TM, D_MODEL), lambda b, i: (b, jnp.minimum(i, N_LAT_TM - 1), 0)),
        pl.BlockSpec((1, CTX_LEN, D_MODEL), lambda b, i: (b, 0, 0)),
        pl.BlockSpec((1, 1, 3 * D_MODEL), lambda b, i: (jnp.where(i < N_LAT_TM, b, BATCH), 0, 0)),
        pl.BlockSpec((D_MODEL, n_in), lambda b, i: (0, 0)),
        pl.BlockSpec((1, LANES), lambda b, i: (0, 0)),
        pl.BlockSpec((1, LANES), lambda b, i: (0, 0)),
        pl.BlockSpec((TM, LANES), lambda b, i: (i, 0)),
        pl.BlockSpec((TM, LANES), lambda b, i: (i, 0)),
        pl.BlockSpec((LANES, LANES), lambda b, i: (0, 0)),
    ]
    transposed = {o[0] for _, outs in plan for o in outs if o[6]}
    out_shape, out_specs = [], []
    for idx, wd in enumerate(out_widths):
        if idx in transposed:
            out_shape.append(jax.ShapeDtypeStruct((BATCH, wd // LANES, TOK // VT, LANES, VT), BF16))
            out_specs.append(pl.BlockSpec((1, wd // LANES, TM // VT, LANES, VT),
                                          lambda b, i: (b, 0, i, 0, 0)))
        else:
            out_shape.append(jax.ShapeDtypeStruct((BATCH, TOK, wd), BF16))
            out_specs.append(pl.BlockSpec((1, TM, wd), lambda b, i: (b, i, 0)))
    return pl.pallas_call(
        functools.partial(_proj_kernel, plan=plan),
        out_shape=tuple(out_shape),
        grid=(BATCH, n_tiles),
        in_specs=in_specs,
        out_specs=tuple(out_specs),
        compiler_params=pltpu.CompilerParams(
            dimension_semantics=("arbitrary", "arbitrary"), vmem_limit_bytes=VMEM_LIMIT),
        name="in_proj",
    )(x, xc, mod, w, qg, kg, cos, sin, seg)


def _out_kernel(*refs, final):
    x_ref, y1_ref, y2_ref, z_ref, w_ref, mod_ref = refs[:6]
    if final:
        fn_ref, o_ref = refs[6:]
    else:
        (o_ref,) = refs[6:]
    y = jnp.concatenate([y1_ref[0], y2_ref[0]], axis=-1).astype(F32)
    g = (y * _silu(z_ref[0].astype(F32))).astype(BF16)
    d = jnp.dot(g, w_ref[...], preferred_element_type=F32)
    gate = mod_ref[0, :, 2 * D_MODEL:3 * D_MODEL]
    xo = x_ref[0] + gate * d
    if final:
        ms = jnp.mean(xo * xo, axis=-1, keepdims=True)
        xo = xo * lax.rsqrt(ms + EPS) * fn_ref[...]
    o_ref[0] = xo


def _out_proj(x, y1, y2, z, w, mod, final_gain, *, context=False):
    final = final_gain is not None
    n_tok = CTX_LEN if context else SEQ
    tm = TM if context else TM_OUT
    row0 = SEQ // tm if context else 0
    half = y1.shape[-1]

    def mod_map(b, i):
        return (BATCH if context else b), 0, 0

    in_specs = [
        pl.BlockSpec((1, tm, D_MODEL), lambda b, i: (b, i, 0)),
        pl.BlockSpec((1, tm, half), lambda b, i: (b, row0 + i, 0)),
        pl.BlockSpec((1, tm, half), lambda b, i: (b, row0 + i, 0)),
        pl.BlockSpec((1, tm, 2 * half), lambda b, i: (b, row0 + i, 0)),
        pl.BlockSpec((2 * half, D_MODEL), lambda b, i: (0, 0)),
        pl.BlockSpec((1, 1, 3 * D_MODEL), mod_map),
    ]
    args = (x, y1, y2, z, w, mod)
    if final:
        in_specs.append(pl.BlockSpec((1, D_MODEL), lambda b, i: (0, 0)))
        args = args + (final_gain,)
    return pl.pallas_call(
        functools.partial(_out_kernel, final=final),
        out_shape=jax.ShapeDtypeStruct((BATCH, n_tok, D_MODEL), F32),
        grid=(BATCH, n_tok // tm),
        in_specs=in_specs,
        out_specs=pl.BlockSpec((1, tm, D_MODEL), lambda b, i: (b, i, 0)),
        compiler_params=pltpu.CompilerParams(
            dimension_semantics=("arbitrary", "arbitrary"), vmem_limit_bytes=VMEM_LIMIT),
        name="out_proj",
    )(*args)


def _write_qt(q_ref, qt_scr, nq, tq):
    feat = lax.broadcasted_iota(jnp.int32, (LANES, VT), 0)
    for j in range(nq):
        for u in range(tq // VT):
            qt = q_ref[0, j, u].astype(F32)
            lo = (2 * j) * tq + u * VT
            hi = (2 * j + 1) * tq + u * VT
            qt_scr[:, lo:lo + VT] = jnp.where(feat < HALF, qt, 0.0).astype(BF16)
            qt_scr[:, hi:hi + VT] = jnp.where(feat >= HALF, qt, 0.0).astype(BF16)


def _qt_spec(nq, tq, index_map):
    return pl.BlockSpec((1, nq, tq // VT, LANES, VT), index_map)


def _merge_qt(ot, nq, tq):
    feat = lax.broadcasted_iota(jnp.int32, (LANES, tq), 0)
    pieces = []
    for j in range(nq):
        blk = jnp.where(feat < HALF, ot[:, (2 * j) * tq:(2 * j + 1) * tq],
                        ot[:, (2 * j + 1) * tq:(2 * j + 2) * tq])
        pieces.append(blk.T.astype(BF16))
    return pieces[0] if nq == 1 else jnp.concatenate(pieces, axis=-1)


def _dense_kernel(*refs, nq, tq, tk, n_lat_tiles, diff, lam0):
    if diff:
        q_ref, k_ref, vt_ref, lamv_ref, subln_ref, o_ref, qt_scr, m_scr, l_scr, acc_scr, s_scr = refs
    else:
        q_ref, k_ref, vt_ref, o_ref, qt_scr, m_scr, l_scr, acc_scr, s_scr = refs
    n_q = 2 * nq * tq
    _write_qt(q_ref, qt_scr, nq, tq)
    m_scr[...] = jnp.full((1, n_q), -jnp.inf, F32)
    l_scr[...] = jnp.zeros((1, n_q), F32)
    acc_scr[...] = jnp.zeros((LANES, n_q), F32)

    n_blk = n_q // QB
    ctx_chunk = (SEQ, SEQ // VT, CTX_LEN // VT)
    lat_chunks = [(c * tk, c * (tk // VT), tk // VT) for c in range(SEQ // tk)]

    def scores(chunk, n, half):
        start, _, subs = chunk
        kc = k_ref[0, pl.ds(start, subs * VT), :]
        s_scr[half, n, 0:subs * VT, :] = jnp.dot(kc, qt_scr[:, n * QB:(n + 1) * QB],
                                                 preferred_element_type=F32)

    def softmax_pv(chunk, n, half):
        _, tile0, subs = chunk
        cols = slice(n * QB, (n + 1) * QB)
        vtc = jnp.concatenate([vt_ref[0, 0, tile0 + u] for u in range(subs)], axis=1)
        s = s_scr[half, n, 0:subs * VT, :]
        m_prev = m_scr[:, cols]
        m_new = jnp.maximum(m_prev, jnp.max(s, axis=0, keepdims=True))
        alpha = jnp.exp2(m_prev - m_new)
        p = jnp.exp2(s - m_new)
        l_scr[:, cols] = alpha * l_scr[:, cols] + jnp.sum(p, axis=0, keepdims=True)
        acc_scr[:, cols] = alpha * acc_scr[:, cols] + jnp.dot(
            vtc, p.astype(BF16), preferred_element_type=F32)
        m_scr[:, cols] = m_new

    def half_step(cur, half, nxt):
        for n in range(n_blk):
            if nxt is not None:
                scores(nxt, n, 1 - half)
            softmax_pv(cur, n, half)

    def latent_keys():
        half_step(ctx_chunk, 0, lat_chunks[0])
        for c, chunk in enumerate(lat_chunks):
            half_step(chunk, (c + 1) % 2, lat_chunks[c + 1] if c + 1 < len(lat_chunks) else None)

    def context_keys_only():
        half_step(ctx_chunk, 0, None)

    for n in range(n_blk):
        scores(ctx_chunk, n, 0)
    is_latent = pl.program_id(2) < n_lat_tiles
    pl.when(is_latent)(latent_keys)
    pl.when(jnp.logical_not(is_latent))(context_keys_only)

    ot = acc_scr[...] * (1.0 / l_scr[...])
    if diff:
        lv = lamv_ref[...]
        lam = (jnp.exp(jnp.sum(lv[0:1] * lv[1:2], axis=-1, keepdims=True))
               - jnp.exp(jnp.sum(lv[2:3] * lv[3:4], axis=-1, keepdims=True)) + lam0)
        dlt = (ot[:, 0:tq] - lam * ot[:, tq:2 * tq]).T
        ms = jnp.mean(dlt * dlt, axis=-1, keepdims=True)
        y = dlt * lax.rsqrt(ms + EPS) * subln_ref[...] * (1.0 - lam0)
        o_ref[0] = y.astype(BF16)
    else:
        o_ref[0] = _merge_qt(ot, nq, tq)


def _dense_attn(q, k, vt, *, nq, tq, tk, n_q_tok, diff=False, lamv=None, subln=None, lam0=0.0):
    groups = q.shape[1] // nq
    n_tiles = n_q_tok // tq
    n_lat_tiles = SEQ // tq
    n_q = 2 * nq * tq
    in_specs = [
        _qt_spec(nq, tq, lambda b, g, i: (b, g, i, 0, 0)),
        pl.BlockSpec((1, TOK, LANES), lambda b, g, i: (b, 0, g)),
        pl.BlockSpec((1, 1, TOK // VT, LANES, VT), lambda b, g, i: (b, g, 0, 0, 0)),
    ]
    args = (q, k, vt)
    if diff:
        in_specs += [pl.BlockSpec((4, HEAD_DIM), lambda b, g, i: (0, 0)),
                     pl.BlockSpec((1, LANES), lambda b, g, i: (0, 0))]
        args = args + (lamv, subln)
    return pl.pallas_call(
        functools.partial(_dense_kernel, nq=nq, tq=tq, tk=tk, n_lat_tiles=n_lat_tiles,
                          diff=diff, lam0=lam0),
        out_shape=jax.ShapeDtypeStruct((BATCH, n_q_tok, groups * nq * LANES), BF16),
        grid=(BATCH, groups, n_tiles),
        in_specs=in_specs,
        out_specs=pl.BlockSpec((1, tq, nq * LANES), lambda b, g, i: (b, i, g)),
        scratch_shapes=[pltpu.VMEM((LANES, n_q), BF16),
                        pltpu.VMEM((1, n_q), F32),
                        pltpu.VMEM((1, n_q), F32),
                        pltpu.VMEM((LANES, n_q), F32),
                        pltpu.VMEM((2, n_q // QB, tk, QB), F32)],
        compiler_params=pltpu.CompilerParams(
            dimension_semantics=("arbitrary", "arbitrary", "arbitrary"),
            vmem_limit_bytes=VMEM_LIMIT),
        name="dense_diff_attn" if diff else "dense_gqa_attn",
    )(*args)


def _nb_kernel(q_ref, k_ref, vt_ref, bias_ref, o_ref, qt_scr, s_scr):
    tq = TQ_BLK
    nq = B_HEADS // 2
    n_keys = NB_KEYS + CTX_LEN
    first = jnp.clip(pl.program_id(1) - 2, 0, N_LAT_TQ - NB_BLKS)
    start = pl.multiple_of(first * tq, tq)
    _write_qt(q_ref, qt_scr, nq, tq)
    for j in range(nq):
        cs = slice(j * LANES, (j + 1) * LANES)
        kcat = jnp.concatenate([k_ref[0, pl.ds(start, NB_KEYS), cs], k_ref[0, SEQ:TOK, cs]], axis=0)
        s_scr[j] = jnp.dot(kcat, qt_scr[:, (2 * j) * tq:(2 * j + 2) * tq], preferred_element_type=F32)
    outs = []
    for j in range(nq):
        vtcat = jnp.concatenate([vt_ref[0, j, first + u] for u in range(NB_BLKS)]
                                + [vt_ref[0, j, SEQ // VT + u] for u in range(CTX_LEN // VT)],
                                axis=1)
        s_n = s_scr[j, 0:NB_KEYS, :] + bias_ref[0, j]
        s_c = s_scr[j, NB_KEYS:n_keys, :]
        m = jnp.maximum(jnp.max(s_n, axis=0, keepdims=True), jnp.max(s_c, axis=0, keepdims=True))
        p = jnp.concatenate([jnp.exp2(s_n - m), jnp.exp2(s_c - m)], axis=0)
        l = jnp.sum(p, axis=0, keepdims=True)
        outs.append(jnp.dot(vtcat, p.astype(BF16), preferred_element_type=F32) * (1.0 / l))
    o_ref[0] = _merge_qt(jnp.concatenate(outs, axis=1), nq, tq)


def _nb_case(i):
    last = N_LAT_TQ - 1
    return jnp.where(i == 0, 0, jnp.where(i == 1, 1, jnp.where(
        i == last - 1, 3, jnp.where(i == last, 4, jnp.where(i > last, 5, 2)))))


def _nb_attn(q, k, vt, bias):
    width = B_HEADS * HEAD_DIM
    n_pairs = B_HEADS // 2
    n_tiles = N_LAT_TQ + N_CTX_TQ

    in_specs = [_qt_spec(n_pairs, TQ_BLK, lambda b, i: (b, 0, i, 0, 0)),
                pl.BlockSpec((1, TOK, width), lambda b, i: (b, 0, 0), pipeline_mode=pl.Buffered(1)),
                pl.BlockSpec((1, n_pairs, TOK // VT, LANES, VT), lambda b, i: (b, 0, 0, 0, 0),
                             pipeline_mode=pl.Buffered(1)),
                pl.BlockSpec((1, n_pairs, NB_KEYS, 2 * TQ_BLK), lambda b, i: (_nb_case(i), 0, 0, 0))]
    return pl.pallas_call(
        _nb_kernel,
        out_shape=jax.ShapeDtypeStruct((BATCH, TOK, width), BF16),
        grid=(BATCH, n_tiles),
        in_specs=in_specs,
        out_specs=pl.BlockSpec((1, TQ_BLK, width), lambda b, i: (b, i, 0)),
        scratch_shapes=[pltpu.VMEM((LANES, B_HEADS * TQ_BLK), BF16),
                        pltpu.VMEM((n_pairs, NB_KEYS + CTX_LEN, 2 * TQ_BLK), F32)],
        compiler_params=pltpu.CompilerParams(
            dimension_semantics=("arbitrary", "arbitrary"), vmem_limit_bytes=VMEM_LIMIT),
        name="neighbourhood_attn",
    )(q, k, vt, bias)


def _nb_bias_table(rpb):
    grid_spec = pltpu.PrefetchScalarGridSpec(
        num_scalar_prefetch=1,
        grid=(6, B_HEADS),
        in_specs=[],
        out_specs=pl.BlockSpec((1, 1, NB_KEYS, TQ_BLK), lambda c, h, r: (c, h // 2, 0, h % 2)),
    )
    return pl.pallas_call(
        _nb_bias_kernel,
        out_shape=jax.ShapeDtypeStruct((6, B_HEADS // 2, NB_KEYS, 2 * TQ_BLK), F32),
        grid_spec=grid_spec,
        compiler_params=pltpu.CompilerParams(dimension_semantics=("arbitrary", "arbitrary")),
        name="nb_bias_table",
    )(rpb.astype(F32).reshape(-1))


def _nb_bias_kernel(rpb_ref, o_ref):
    c = pl.program_id(0)
    h = pl.program_id(1)
    last = N_LAT_TQ - 1
    tile = jnp.where(c == 3, last - 1, jnp.where(c == 4, last, jnp.minimum(c, 2)))
    r0 = 2 * tile
    base = jnp.clip(2 * tile - 4, 0, GRID_ROWS - NB_ROWS)
    n_dr = 2 * NA_ROWS - 1
    n_dc = 2 * NA_COLS - 1
    kc = lax.broadcasted_iota(jnp.int32, (GRID_W, LANES), 0)
    lane = lax.broadcasted_iota(jnp.int32, (GRID_W, LANES), 1)
    qc = lane & (GRID_W - 1)
    second = lane >= GRID_W
    dc = kc - qc + (NA_COLS - 1)
    cs = jnp.clip(qc - NA_COLS // 2, 0, GRID_W - NA_COLS)
    col_ok = jnp.where(kc >= cs, jnp.where(kc < cs + NA_COLS, 1, 0), 0)
    rs0 = jnp.clip(r0 - NA_ROWS // 2, 0, GRID_ROWS - NA_ROWS)
    rs1 = jnp.clip(r0 + 1 - NA_ROWS // 2, 0, GRID_ROWS - NA_ROWS)
    for krow in range(NB_ROWS):
        kr = base + krow
        dr = kr - r0 + (NA_ROWS - 1)
        ok0 = jnp.where((kr >= rs0) & (kr < rs0 + NA_ROWS) & (c < 5), 1, 0)
        ok1 = jnp.where((kr >= rs1) & (kr < rs1 + NA_ROWS) & (c < 5), 1, 0)
        off0 = (h * n_dr + jnp.clip(dr, 0, n_dr - 1)) * n_dc
        off1 = (h * n_dr + jnp.clip(dr - 1, 0, n_dr - 1)) * n_dc
        acc = jnp.zeros((GRID_W, LANES), F32)
        for j in range(n_dc):
            val = jnp.where(second, rpb_ref[off1 + j], rpb_ref[off0 + j])
            acc = jnp.where(dc == j, val, acc)
        ok = col_ok * jnp.where(second, ok1, ok0)
        o_ref[0, 0, krow * GRID_W:(krow + 1) * GRID_W, :] = jnp.where(ok > 0, acc * LOG2E, NEG_INF)


def _win_kernel(sink_ref, q_ref, k_ref, vt_ref, o_ref, qt_scr, s_scr):
    tq = TQ_BLK
    nq = D_HEADS // 2
    n_blk = 2 * nq * tq // QB
    i = pl.program_id(1)
    first = jnp.clip(i - D_WINDOW // tq, 0, N_LAT_TQ - WIN_BLKS)
    start = pl.multiple_of(first * tq, tq)
    _write_qt(q_ref, qt_scr, nq, tq)
    kcat = jnp.concatenate([k_ref[0, pl.ds(start, WIN_BLKS * tq), :], k_ref[0, SEQ:TOK, :]],
                           axis=0)
    vtcat = jnp.concatenate([vt_ref[0, 0, first + u] for u in range(WIN_BLKS)]
                            + [vt_ref[0, 0, SEQ // VT + u] for u in range(CTX_LEN // VT)],
                            axis=1)
    for n in range(n_blk):
        s_scr[n] = jnp.dot(kcat, qt_scr[:, n * QB:(n + 1) * QB], preferred_element_type=F32)

    span = WIN_BLKS * tq
    n_keys = span + CTX_LEN
    kk = lax.broadcasted_iota(jnp.int32, (n_keys, QB), 0)
    qq = lax.broadcasted_iota(jnp.int32, (n_keys, QB), 1) & (tq - 1)
    rel = kk - qq + (first - i) * tq
    mask = jnp.where(kk >= span, 0.0, jnp.where(jnp.abs(rel) <= D_WINDOW, 0.0, NEG_INF))

    outs = []
    for n in range(n_blk):
        sinks = jnp.concatenate(
            [jnp.full((1, tq), sink_ref[HEAD_PERM[n * QB // tq + g]], F32) for g in range(QB // tq)], axis=1)
        s = s_scr[n] + mask
        m = jnp.maximum(jnp.max(s, axis=0, keepdims=True), sinks)
        p = jnp.exp2(s - m)
        l = jnp.sum(p, axis=0, keepdims=True) + jnp.exp2(sinks - m)
        outs.append(jnp.dot(vtcat, p.astype(BF16), preferred_element_type=F32) * (1.0 / l))
    o_ref[0] = _merge_qt(jnp.concatenate(outs, axis=1), nq, tq)


def _win_attn(q, k, vt, sinks):
    width = D_HEADS * HEAD_DIM
    n_q = D_HEADS * TQ_BLK
    grid_spec = pltpu.PrefetchScalarGridSpec(
        num_scalar_prefetch=1,
        grid=(BATCH, N_LAT_TQ),
        in_specs=[_qt_spec(D_HEADS // 2, TQ_BLK, lambda b, i, s: (b, 0, i, 0, 0)),
                  pl.BlockSpec((1, TOK, LANES), lambda b, i, s: (b, 0, 0)),
                  pl.BlockSpec((1, 1, TOK // VT, LANES, VT), lambda b, i, s: (b, 0, 0, 0, 0))],
        out_specs=pl.BlockSpec((1, TQ_BLK, width), lambda b, i, s: (b, i, 0)),
        scratch_shapes=[pltpu.VMEM((LANES, n_q), BF16),
                        pltpu.VMEM((n_q // QB, WIN_BLKS * TQ_BLK + CTX_LEN, QB), F32)],
    )
    return pl.pallas_call(
        _win_kernel,
        out_shape=jax.ShapeDtypeStruct((BATCH, SEQ, width), BF16),
        grid_spec=grid_spec,
        compiler_params=pltpu.CompilerParams(
            dimension_semantics=("arbitrary", "arbitrary"), vmem_limit_bytes=VMEM_LIMIT),
        name="window_attn",
    )(sinks, q, k, vt)


def _rope_tables():
    t = jnp.arange(SEQ, dtype=jnp.int32)
    row = (t // GRID_W).astype(F32)
    col = (t % GRID_W).astype(F32)
    quarter = HEAD_DIM // 4
    inv_freq = ROPE_THETA ** (-jnp.arange(quarter, dtype=F32) / quarter)
    ang_r = row[:, None] * inv_freq
    ang_c = col[:, None] * inv_freq
    cr, sr, cc, sc = jnp.cos(ang_r), jnp.sin(ang_r), jnp.cos(ang_c), jnp.sin(ang_c)
    cos = jnp.tile(jnp.concatenate([cr, cr, cc, cc], axis=-1), (1, LANES // HEAD_DIM))
    sin = jnp.tile(jnp.concatenate([-sr, sr, -sc, sc], axis=-1), (1, LANES // HEAD_DIM))
    cos = jnp.concatenate([cos, jnp.ones((CTX_LEN, LANES), F32)], axis=0)
    sin = jnp.concatenate([sin, jnp.zeros((CTX_LEN, LANES), F32)], axis=0)
    return cos, sin


def _head_cols(perm):
    return np.concatenate([np.arange(HEAD_DIM) + HEAD_DIM * h for h in perm])


def _take_static(w, idx, axis):
    breaks = np.flatnonzero(np.diff(idx) != 1) + 1
    runs = np.split(idx, breaks)
    parts = [lax.slice_in_dim(w, int(r[0]), int(r[-1]) + 1, axis=axis) for r in runs]
    return parts[0] if len(parts) == 1 else jnp.concatenate(parts, axis=axis)


def kernel(x, c, ctx, c_ctx, w_mod, b_mod, w_in_even, w_out_even, a_q_norm, a_k_norm, b_rpb,
           w_in_odd, w_out_odd, c_lambda, c_subln, d_sinks, final_norm):
    assert DEPTH == 2 and x.shape == (BATCH, SEQ, D_MODEL) and ctx.shape == (BATCH, CTX_LEN, D_MODEL)
    perm = _head_cols(HEAD_PERM)
    hw = A_HEADS * HEAD_DIM

    cs = jnp.concatenate([c, c_ctx[None], jnp.zeros((8 - BATCH - 1, D_MODEL), F32)], axis=0)
    mod = _modulation(cs, w_mod, b_mod)
    mod0 = mod[0].reshape(8, 1, 3 * D_MODEL)
    mod1 = mod[1].reshape(8, 1, 3 * D_MODEL)

    cos, sin = _rope_tables()
    seg = jnp.asarray(np.kron(np.eye(LANES // HEAD_DIM), np.ones((HEAD_DIM, HEAD_DIM))), BF16)
    tile2 = LANES // HEAD_DIM
    qg = jnp.tile(a_q_norm[0].astype(F32), tile2).reshape(1, LANES)
    kg = jnp.tile(a_k_norm[0].astype(F32), tile2).reshape(1, LANES)

    z0 = 2304
    cols0 = np.concatenate([perm, np.arange(hw, z0), z0 + perm, np.arange(z0 + hw, z0 + 2 * hw)])
    w_in0 = _take_static(w_in_even[0], cols0, 1).astype(BF16)
    w_out0 = _take_static(w_out_even[0], np.concatenate([perm, np.arange(hw, 2 * hw)]), 0).astype(BF16)
    plan0 = (
        (hw, ((0, 0, hw, "q", True, True, True),)),
        (2 * LANES, ((1, 0, LANES, "k", True, False, False),
                     (2, LANES, 2 * LANES, None, False, False, True))),
        (hw, ((3, 0, hw, None, False, True, True),)),
        (hw, ((4, 0, hw, None, False, False, False),)),
        (hw, ((5, 0, hw, None, False, False, True),)),
        (2 * hw, ((6, 0, 2 * hw, None, False, False, False),)),
    )
    qa, ka, va, qb, kb, vb, zz = _in_proj(x, ctx, mod0, w_in0, qg, kg, cos, sin, seg, plan0,
                                          (hw, LANES, LANES, hw, hw, hw, 2 * hw))
    ya = _dense_attn(qa, ka, va, nq=4, tq=256, tk=DENSE_TK, n_q_tok=TOK)
    yb = _nb_attn(qb, kb, vb, _nb_bias_table(b_rpb[0]))
    x1 = _out_proj(x, ya, yb, zz, w_out0, mod0, None)
    xc1 = _out_proj(ctx, ya, yb, zz, w_out0, mod0, None, context=True)

    cols1 = np.concatenate([np.arange(0, 3 * hw), 3 * hw + perm, np.arange(4 * hw, z0 + hw), z0 + hw + perm])
    w_in1 = _take_static(w_in_odd[0], cols1, 1).astype(BF16)
    w_out1 = _take_static(w_out_odd[0], np.concatenate([np.arange(hw), hw + perm]), 0).astype(BF16)
    plan1 = (
        (hw, ((0, 0, hw, None, True, True, True),)),
        (hw, ((1, 0, hw, None, True, False, False),)),
        (hw, ((2, 0, hw, None, False, False, True),)),
        (hw, ((3, 0, hw, None, True, True, True),)),
        (2 * LANES, ((4, 0, LANES, None, True, False, False),
                     (5, LANES, 2 * LANES, None, False, False, True))),
        (2 * hw, ((6, 0, 2 * hw, None, False, False, False),)),
    )
    qc, kc, vc, qd, kd, vd, zz1 = _in_proj(x1, xc1, mod1, w_in1, qg, kg, cos, sin, seg, plan1,
                                           (hw, hw, hw, hw, LANES, LANES, 2 * hw))
    yc = _dense_attn(qc, kc, vc, nq=1, tq=1024, tk=DENSE_TK, n_q_tok=SEQ, diff=True,
                     lamv=c_lambda[0].astype(F32), subln=c_subln[0].astype(F32).reshape(1, LANES),
                     lam0=_lambda_init(1))
    yd = _win_attn(qd, kd, vd, d_sinks[0].astype(F32) * LOG2E)
    return _out_proj(x1, yc, yd, zz1, w_out1, mod1, final_norm.astype(F32).reshape(1, D_MODEL))
```
